```python
import jax, jax.numpy as jnp
from jax import lax
import numpy as np

D_MODEL = 1024
BATCH = 2
SEQ = 8192
DEPTH = 4
DEC_BATCH = 128
DEC_SEQ = 8
PAST_LEN = 2048
PAGE_SIZE = 128

HEAD_DIM = 64
H_FOX = 8
H_DSA = 8
W_FOX = H_FOX * HEAD_DIM
W_DSA = H_DSA * HEAD_DIM
H_IDX = 8
D_IDX = 64
TOPK_MAX = 256
ROPE_THETA = 500000.0
ROPE_DIM = HEAD_DIM // 4
H_RWKV = D_MODEL // HEAD_DIM
LORA_DECAY = 64
LORA_A = 64
LORA_V = 32
D_PLE = 256
Q_BLOCK = 128
RMS_EPS = 1e-6
GN_EPS = 64e-5
N_EVEN = (DEPTH + 1) // 2
N_ODD = DEPTH // 2
EVEN_SIZES = (W_FOX, W_FOX, W_FOX, H_FOX, W_FOX, W_DSA, W_DSA, W_DSA, W_DSA, H_IDX * D_IDX, D_IDX, H_IDX)
EVEN_CUTS = tuple(int(c) for c in np.cumsum(EVEN_SIZES)[:-1])
EVEN_IN = sum(EVEN_SIZES)

kernel_name = 'hybrid_fox_dsa_rwkv7_step'


def rmsnorm(x, g):
    xf = x.astype(jnp.float32)
    y = xf * lax.rsqrt(jnp.mean(xf * xf, axis=-1, keepdims=True) + RMS_EPS)
    return (y * g.astype(jnp.float32)).astype(x.dtype)


def partial_rope(x, pos):
    half = ROPE_DIM // 2
    inv = ROPE_THETA ** (-2.0 * jnp.arange(half, dtype=jnp.float32) / ROPE_DIM)
    ang = pos.astype(jnp.float32)[:, None] * inv
    cos, sin = jnp.cos(ang)[:, None, :], jnp.sin(ang)[:, None, :]
    xf = x.astype(jnp.float32)
    x1, x2 = xf[..., :half], xf[..., half:ROPE_DIM]
    out = jnp.concatenate([x1 * cos - x2 * sin, x2 * cos + x1 * sin, xf[..., ROPE_DIM:]], axis=-1)
    return out.astype(x.dtype)


def even_project(hn, w_in, b_forget, pos):
    B, T, _ = hn.shape
    fq, fk, fv, ff, fg, dq, dk, dv, dg, iq, ik, iw = jnp.split(hn @ w_in, EVEN_CUTS, axis=-1)
    def heads(t, h):
        return t.reshape(B, T, h, -1)
    logf = jax.nn.log_sigmoid((ff + b_forget).astype(jnp.float32))
    dq = partial_rope(heads(dq, H_DSA), pos)
    dk = partial_rope(heads(dk, H_DSA), pos)
    iq = partial_rope(heads(iq, H_IDX), pos)
    ik = partial_rope(ik[:, :, None, :], pos)[:, :, 0]
    iw = iw * (H_IDX ** -0.5)
    return (heads(fq, H_FOX), heads(fk, H_FOX), heads(fv, H_FOX), logf, jax.nn.silu(fg),
            dq, dk, heads(dv, H_DSA), jax.nn.silu(dg), iq, ik, iw)


def gather_rows(rows, idx):
    return jax.vmap(lambda r, i: r[i])(rows, idx)


def gather_pages(cache, page_table, layer):
    g = cache[page_table, layer]
    return g.reshape((g.shape[0], g.shape[1] * g.shape[2]) + g.shape[3:])


def fox_prompt(q, k, v, logf):
    B, S, H, dh = q.shape
    nb = S // Q_BLOCK
    scale = HEAD_DIM ** -0.5
    c = jnp.cumsum(logf, axis=1)
    c_keys = jnp.swapaxes(c, 1, 2)[:, :, None, :]
    kpos = jnp.arange(S)
    def block(args):
        qb, cb, start = args
        qpos = start + jnp.arange(Q_BLOCK)
        s = jnp.einsum('bqhd,bshd->bhqs', qb, k, preferred_element_type=jnp.float32) * scale
        s = s + jnp.swapaxes(cb, 1, 2)[..., None] - c_keys
        s = jnp.where(kpos[None, :] <= qpos[:, None], s, -jnp.inf)
        p = jax.nn.softmax(s, axis=-1).astype(v.dtype)
        return jnp.einsum('bhqs,bshd->bqhd', p, v)
    qb = jnp.swapaxes(q.reshape(B, nb, Q_BLOCK, H, dh), 0, 1)
    cb = jnp.swapaxes(c.reshape(B, nb, Q_BLOCK, H), 0, 1)
    out = lax.map(block, (qb, cb, jnp.arange(nb) * Q_BLOCK))
    return jnp.swapaxes(out, 0, 1).reshape(B, S, H, dh)


def fox_sample(q, k, v, logf, k_past, v_past, logf_past):
    T = q.shape[1]
    P = k_past.shape[1]
    scale = HEAD_DIM ** -0.5
    c_past = jnp.cumsum(logf_past.astype(jnp.float32), axis=1)
    c_new = c_past[:, -1:] + jnp.cumsum(logf, axis=1)
    cq = jnp.swapaxes(c_new, 1, 2)[..., None]
    s_p = jnp.einsum('bqhd,bshd->bhqs', q, k_past, preferred_element_type=jnp.float32) * scale
    s_p = s_p + cq - jnp.swapaxes(c_past, 1, 2)[:, :, None, :]
    s_n = jnp.einsum('bqhd,bshd->bhqs', q, k, preferred_element_type=jnp.float32) * scale
    s_n = s_n + cq - jnp.swapaxes(c_new, 1, 2)[:, :, None, :]
    s_n = jnp.where(jnp.arange(T)[None, :] <= jnp.arange(T)[:, None], s_n, -jnp.inf)
    p = jax.nn.softmax(jnp.concatenate([s_p, s_n], axis=-1), axis=-1).astype(v.dtype)
    return (jnp.einsum('bhqs,bshd->bqhd', p[..., :P], v_past)
            + jnp.einsum('bhqs,bshd->bqhd', p[..., P:], v))


def indexer_scores(qidx, widx, kidx):
    dots = jax.nn.relu(jnp.einsum('bqhd,bld->bqhl', qidx, kidx, preferred_element_type=jnp.float32))
    return jnp.einsum('bqhl,bqh->bql', dots, widx.astype(jnp.float32)) * (D_IDX ** -0.5)


def sparse_attend(q, ks, vs, valid):
    s = jnp.einsum('bqhd,bqkhd->bhqk', q, ks, preferred_element_type=jnp.float32) * (HEAD_DIM ** -0.5)
    s = jnp.where(valid[:, None], s, -jnp.inf)
    p = jax.nn.softmax(s, axis=-1).astype(vs.dtype)
    return jnp.einsum('bhqk,bqkhd->bqhd', p, vs)


def dsa_prompt(q, k, v, qidx, kidx, widx):
    B, S = q.shape[:2]
    topk = min(TOPK_MAX, S // 4)
    nb = S // Q_BLOCK
    kpos = jnp.arange(S)
    def blocks(a):
        return jnp.swapaxes(a.reshape((B, nb, Q_BLOCK) + a.shape[2:]), 0, 1)
    def block(args):
        qb, qib, wib, start = args
        qpos = start + jnp.arange(Q_BLOCK)
        sc = indexer_scores(qib, wib, kidx)
        sc = jnp.where((kpos[None, :] <= qpos[:, None])[None], sc, -jnp.inf)
        _, sel = lax.top_k(sc, topk)
        valid = sel <= qpos[None, :, None]
        return sparse_attend(qb, gather_rows(k, sel), gather_rows(v, sel), valid)
    out = lax.map(block, (blocks(q), blocks(qidx), blocks(widx), jnp.arange(nb) * Q_BLOCK))
    return jnp.swapaxes(out, 0, 1).reshape(q.shape)


def dsa_sample(q, k, v, qidx, kidx, widx, cache_k, cache_v, cache_kidx, page_table, layer):
    DB, T = q.shape[:2]
    P = page_table.shape[1] * PAGE_SIZE
    topk = min(TOPK_MAX, (P + T) // 4)
    kidx_all = jnp.concatenate([gather_pages(cache_kidx, page_table, layer), kidx], axis=1)
    qpos = P + jnp.arange(T)
    kpos = jnp.arange(P + T)
    sc = indexer_scores(qidx, widx, kidx_all)
    sc = jnp.where((kpos[None, :] <= qpos[:, None])[None], sc, -jnp.inf)
    _, sel = lax.top_k(sc, topk)
    valid = sel <= qpos[None, :, None]
    from_past = valid & (sel < P)
    sp = jnp.minimum(sel, P - 1)
    phys = jnp.take_along_axis(page_table, (sp // PAGE_SIZE).reshape(DB, -1), axis=1).reshape(sel.shape)
    off = sp % PAGE_SIZE
    k_sel = cache_k[phys, layer, off]
    v_sel = cache_v[phys, layer, off]
    new_pos = P + jnp.arange(T)
    from_new = jnp.any(sel[..., None] == new_pos, axis=-2) & (new_pos[None, :] <= qpos[:, None])[None]
    scale = HEAD_DIM ** -0.5
    s_p = jnp.einsum('bqhd,bqkhd->bhqk', q, k_sel, preferred_element_type=jnp.float32) * scale
    s_n = jnp.einsum('bqhd,bshd->bhqs', q, k, preferred_element_type=jnp.float32) * scale
    s = jnp.concatenate([jnp.where(from_past[:, None], s_p, -jnp.inf),
                         jnp.where(from_new[:, None], s_n, -jnp.inf)], axis=-1)
    p = jax.nn.softmax(s, axis=-1).astype(v.dtype)
    return (jnp.einsum('bhqk,bqkhd->bqhd', p[..., :topk], v_sel)
            + jnp.einsum('bhqs,bshd->bqhd', p[..., topk:], v))


def wkv7_scan(s0, r, decay, k, v, kk, a):
    def step(s, inp):
        r_t, w_t, k_t, v_t, kk_t, a_t = inp
        sa = jnp.einsum('bhvk,bhk->bhv', s, -kk_t)
        s = (s * w_t[:, :, None, :] + sa[..., None] * (kk_t * a_t)[:, :, None, :]
             + v_t[..., None] * k_t[:, :, None, :])
        return s, jnp.einsum('bhvk,bhk->bhv', s, r_t)
    xs = tuple(jnp.swapaxes(t, 0, 1) for t in (r, decay, k, v, kk, a))
    s, ys = lax.scan(step, s0, xs)
    return s, jnp.swapaxes(ys, 0, 1)


def rwkv_mix(xn, x_prev, s0, v_first, vres, mu, w_rkvg, w_o, w_d0, w_d1, w_d2, w_a0, w_a1, w_a2,
             k_k, k_a, r_k, ln_w, ln_b):
    B, T, D = xn.shape
    f32 = jnp.float32
    dx = jnp.concatenate([x_prev[:, None, :].astype(xn.dtype), xn[:, :-1]], axis=1) - xn
    xr, xw, xk, xv, xa, xg = [xn + dx * mu[j] for j in range(6)]
    r = xr @ w_rkvg[0]
    k = xk @ w_rkvg[1]
    v = xv @ w_rkvg[2]
    gate = jax.nn.silu(xg @ w_rkvg[3])
    w_log = -jax.nn.softplus(-(w_d0 + jnp.tanh(xw @ w_d1) @ w_d2).astype(f32)) - 0.5
    decay = jnp.exp(-jnp.exp(w_log))
    a = jax.nn.sigmoid((w_a0 + (xa @ w_a1) @ w_a2).astype(f32))
    if vres is None:
        v_first = v
    else:
        v0, v1, v2 = vres
        v = v + (v_first - v) * jax.nn.sigmoid(v0 + (xv @ v1) @ v2)
    def heads(t):
        return t.reshape(B, T, H_RWKV, HEAD_DIM).astype(f32)
    r_h, k_h, v_h, a_h, w_h = heads(r), heads(k), heads(v), heads(a), heads(decay)
    kk = heads(k * k_k)
    kk = kk * lax.rsqrt(jnp.maximum(jnp.sum(kk * kk, axis=-1, keepdims=True), 1e-24))
    k_h = k_h * (1.0 + (a_h - 1.0) * k_a.reshape(H_RWKV, HEAD_DIM).astype(f32))
    s_new, y = wkv7_scan(s0.astype(f32), r_h, w_h, k_h, v_h, kk, a_h)
    mean = jnp.mean(y, axis=-1, keepdims=True)
    var = jnp.mean(jnp.square(y - mean), axis=-1, keepdims=True)
    y = ((y - mean) * lax.rsqrt(var + GN_EPS) * ln_w.reshape(H_RWKV, HEAD_DIM).astype(f32)
         + ln_b.reshape(H_RWKV, HEAD_DIM).astype(f32))
    y = y + jnp.sum(r_h * k_h * r_k.astype(f32), axis=-1, keepdims=True) * v_h
    out = (y.reshape(B, T, D).astype(xn.dtype) * gate) @ w_o
    return out, s_new.astype(s0.dtype), xn[:, -1], v_first


def run_group(x, ple, ctx, prm):
    B, T, _ = x.shape
    past = 0 if ctx is None else ctx['page_table'].shape[1] * PAGE_SIZE
    pos = past + jnp.arange(T, dtype=jnp.int32)
    h = x
    v_first = None
    even_rows, odd_rows = [], []
    for i in range(DEPTH):
        hn = rmsnorm(h, prm['g_mix'][i])
        if i % 2 == 0:
            e = i // 2
            fq, fk, fv, logf, fg, dq, dk, dv, dg, iq, ik, iw = even_project(hn, prm['w_in'][e], prm['b_forget'][e], pos)
            if ctx is None:
                o_f = fox_prompt(fq, fk, fv, logf)
                o_d = dsa_prompt(dq, dk, dv, iq, ik, iw)
            else:
                pt = ctx['page_table']
                o_f = fox_sample(fq, fk, fv, logf, gather_pages(ctx['cache_k_fox'], pt, e),
                                 gather_pages(ctx['cache_v_fox'], pt, e), gather_pages(ctx['cache_logf_fox'], pt, e))
                o_d = dsa_sample(dq, dk, dv, iq, ik, iw, ctx['cache_k_dsa'], ctx['cache_v_dsa'],
                                 ctx['cache_kidx_dsa'], pt, e)
            mixed = jnp.concatenate([o_f.reshape(B, T, W_FOX) * fg, o_d.reshape(B, T, W_DSA) * dg], axis=-1)
            h = h + mixed @ prm['w_out'][e]
            even_rows.append((fk, fv, logf.astype(x.dtype), dk, dv, ik))
        else:
            o = i // 2
            if ctx is None:
                s0 = jnp.zeros((B, H_RWKV, HEAD_DIM, HEAD_DIM), x.dtype)
                x_prev = jnp.zeros((B, D_MODEL), x.dtype)
            else:
                s0, x_prev = ctx['state_wkv'][o], ctx['state_shift'][o]
            vres = None if o == 0 else (prm['w_v0'][o - 1], prm['w_v1'][o - 1], prm['w_v2'][o - 1])
            out, s_new, shift_new, v_first = rwkv_mix(
                hn, x_prev, s0, v_first, vres, prm['mu_rwkv'][o], prm['w_rkvg'][o], prm['w_o_rwkv'][o],
                prm['w_decay0'][o], prm['w_decay1'][o], prm['w_decay2'][o], prm['w_a0'][o], prm['w_a1'][o],
                prm['w_a2'][o], prm['k_k'][o], prm['k_a'][o], prm['r_k'][o], prm['ln_x_w'][o], prm['ln_x_b'][o])
            h = h + out
            odd_rows.append((s_new, shift_new))
        gate = jax.nn.sigmoid(rmsnorm(h, prm['g_ple'][i]) @ prm['w_ple_gate'][i])
        h = h + gate * (ple[i] @ prm['w_ple_proj'][i])
    y = rmsnorm(h, prm['g_final'])
    new = [jnp.stack([r[j] for r in even_rows], axis=1) for j in range(6)]
    new += [jnp.stack([r[j] for r in odd_rows], axis=0) for j in range(2)]
    return y, new


def setup_inputs(seed: int = 0) -> dict:
    key = jax.random.key(seed)
    ks = iter(jax.random.split(key, 48))
    def nrm(shape, scale=1.0):
        return scale * jax.random.normal(next(ks), shape, jnp.float32)
    def uni(shape, lo, hi):
        return jax.random.uniform(next(ks), shape, jnp.float32, lo, hi)
    D = D_MODEL
    n_pages = PAST_LEN // PAGE_SIZE
    n_used = DEC_BATCH * n_pages
    n_pool = n_used + max(1, n_used // 4)
    nv = max(N_ODD - 1, 0)
    return {
        'x_prompt': nrm((BATCH, SEQ, D)),
        'x_sample': nrm((DEC_BATCH, DEC_SEQ, D)),
        'cache_k_fox': nrm((n_pool, N_EVEN, PAGE_SIZE, H_FOX, HEAD_DIM)),
        'cache_v_fox': nrm((n_pool, N_EVEN, PAGE_SIZE, H_FOX, HEAD_DIM)),
        'cache_logf_fox': jax.nn.log_sigmoid(2.5 + nrm((n_pool, N_EVEN, PAGE_SIZE, H_FOX))),
        'cache_k_dsa': nrm((n_pool, N_EVEN, PAGE_SIZE, H_DSA, HEAD_DIM)),
        'cache_v_dsa': nrm((n_pool, N_EVEN, PAGE_SIZE, H_DSA, HEAD_DIM)),
        'cache_kidx_dsa': nrm((n_pool, N_EVEN, PAGE_SIZE, D_IDX)),
        'state_wkv': nrm((N_ODD, DEC_BATCH, H_RWKV, HEAD_DIM, HEAD_DIM), 0.5),
        'state_shift': nrm((N_ODD, DEC_BATCH, D)),
        'page_table': jax.random.permutation(next(ks), n_pool)[:n_used].reshape(DEC_BATCH, n_pages).astype(jnp.int32),
        'p_prompt': nrm((DEPTH, BATCH, SEQ, D_PLE)),
        'p_sample': nrm((DEPTH, DEC_BATCH, DEC_SEQ, D_PLE)),
        'g_mix': 1.0 + nrm((DEPTH, D), 0.02),
        'w_in': nrm((N_EVEN, D, EVEN_IN), D ** -0.5),
        'b_forget': uni((N_EVEN, H_FOX), 1.0, 4.0),
        'w_out': nrm((N_EVEN, W_FOX + W_DSA, D), (W_FOX + W_DSA) ** -0.5),
        'mu_rwkv': uni((N_ODD, 6, D), 0.0, 1.0),
        'w_rkvg': nrm((N_ODD, 4, D, D), D ** -0.5),
        'w_o_rwkv': nrm((N_ODD, D, D), D ** -0.5),
        'w_decay0': uni((N_ODD, D), -6.0, 1.0),
        'w_decay1': nrm((N_ODD, D, LORA_DECAY), D ** -0.5),
        'w_decay2': nrm((N_ODD, LORA_DECAY, D), 0.2 * LORA_DECAY ** -0.5),
        'w_a0': nrm((N_ODD, D), 0.5),
        'w_a1': nrm((N_ODD, D, LORA_A), D ** -0.5),
        'w_a2': nrm((N_ODD, LORA_A, D), 0.2 * LORA_A ** -0.5),
        'w_v0': nrm((nv, D), 0.5),
        'w_v1': nrm((nv, D, LORA_V), D ** -0.5),
        'w_v2': nrm((nv, LORA_V, D), 0.2 * LORA_V ** -0.5),
        'k_k': 0.85 + nrm((N_ODD, D), 0.05),
        'k_a': 1.0 + nrm((N_ODD, D), 0.05),
        'r_k': nrm((N_ODD, H_RWKV, HEAD_DIM), 0.1),
        'ln_x_w': 1.0 + nrm((N_ODD, D), 0.02),
        'ln_x_b': nrm((N_ODD, D), 0.02),
        'g_ple': 1.0 + nrm((DEPTH, D), 0.02),
        'w_ple_gate': nrm((DEPTH, D, D), D ** -0.5),
        'w_ple_proj': nrm((DEPTH, D_PLE, D), D_PLE ** -0.5),
        'g_final': 1.0 + nrm((D,), 0.02),
    }


def reference(x_prompt, x_sample, cache_k_fox, cache_v_fox, cache_logf_fox, cache_k_dsa, cache_v_dsa,
              cache_kidx_dsa, state_wkv, state_shift, page_table, p_prompt, p_sample, g_mix, w_in, b_forget,
              w_out, mu_rwkv, w_rkvg, w_o_rwkv, w_decay0, w_decay1, w_decay2, w_a0, w_a1, w_a2, w_v0, w_v1,
              w_v2, k_k, k_a, r_k, ln_x_w, ln_x_b, g_ple, w_ple_gate, w_ple_proj, g_final):
    prm = {'g_mix': g_mix, 'w_in': w_in, 'b_forget': b_forget, 'w_out': w_out, 'mu_rwkv': mu_rwkv,
           'w_rkvg': w_rkvg, 'w_o_rwkv': w_o_rwkv, 'w_decay0': w_decay0, 'w_decay1': w_decay1,
           'w_decay2': w_decay2, 'w_a0': w_a0, 'w_a1': w_a1, 'w_a2': w_a2, 'w_v0': w_v0, 'w_v1': w_v1,
           'w_v2': w_v2, 'k_k': k_k, 'k_a': k_a, 'r_k': r_k, 'ln_x_w': ln_x_w, 'ln_x_b': ln_x_b,
           'g_ple': g_ple, 'w_ple_gate': w_ple_gate, 'w_ple_proj': w_ple_proj, 'g_final': g_final}
    y_prompt, new_p = run_group(x_prompt, p_prompt, None, prm)
    ctx = {'page_table': page_table, 'cache_k_fox': cache_k_fox, 'cache_v_fox': cache_v_fox,
           'cache_logf_fox': cache_logf_fox, 'cache_k_dsa': cache_k_dsa, 'cache_v_dsa': cache_v_dsa,
           'cache_kidx_dsa': cache_kidx_dsa, 'state_wkv': state_wkv, 'state_shift': state_shift}
    y_sample, new_s = run_group(x_sample, p_sample, ctx, prm)
    k_fox_p, v_fox_p, logf_fox_p, k_dsa_p, v_dsa_p, kidx_dsa_p, wkv_p, shift_p = new_p
    k_fox_s, v_fox_s, logf_fox_s, k_dsa_s, v_dsa_s, kidx_dsa_s, wkv_s, shift_s = new_s
    return (y_prompt, y_sample, k_fox_p, v_fox_p, logf_fox_p, k_dsa_p, v_dsa_p, kidx_dsa_p, wkv_p, shift_p,
            k_fox_s, v_fox_s, logf_fox_s, k_dsa_s, v_dsa_s, kidx_dsa_s, wkv_s, shift_s)
```

```python
import functools

import jax
import jax.numpy as jnp
import numpy as np
from jax import lax
from jax.experimental import pallas as pl
from jax.experimental.pallas import tpu as pltpu

HEAD_DIM = 64
H_FOX = 8
H_DSA = 8
H_IDX = 8
D_IDX = 64
W_FOX = H_FOX * HEAD_DIM
W_DSA = H_DSA * HEAD_DIM
TOPK_MAX = 256
ROPE_THETA = 500000.0
ROPE_DIM = HEAD_DIM // 4
PAGE_SIZE = 128
RMS_EPS = 1e-6
GN_EPS = 64e-5
EVEN_SIZES = (W_FOX, W_FOX, W_FOX, H_FOX, W_FOX, W_DSA, W_DSA, W_DSA, W_DSA, H_IDX * D_IDX, D_IDX, H_IDX)
EVEN_CUTS = tuple(int(c) for c in np.cumsum(EVEN_SIZES)[:-1])

V7X_LANES = 128
V7X_VMEM_LIMIT_BYTES = 56 * 1024 * 1024
NEG_BIG = -1e30
INT_MIN = -2 ** 31

_BF16 = jnp.bfloat16
_F32 = jnp.float32


def _mm_kernel(x_ref, w_ref, o_ref):
    o_ref[...] = jnp.dot(x_ref[...].astype(_BF16), w_ref[...], preferred_element_type=_F32)


def _mm(x, w):
    M, K = x.shape
    N = w.shape[1]
    n_pad = (-N) % V7X_LANES
    wb = w.astype(_BF16)
    if n_pad:
        wb = jnp.pad(wb, ((0, 0), (0, n_pad)))
    Np = N + n_pad
    tm = min(512, M)
    tn = Np
    for cand in (1024, 768, 512, 384, 256, 128):
        if Np % cand == 0:
            tn = cand
            break
    assert M % tm == 0
    out = pl.pallas_call(
        _mm_kernel,
        grid=(M // tm, Np // tn),
        in_specs=[pl.BlockSpec((tm, K), lambda i, j: (i, 0)),
                  pl.BlockSpec((K, tn), lambda i, j: (0, j))],
        out_specs=pl.BlockSpec((tm, tn), lambda i, j: (i, j)),
        out_shape=jax.ShapeDtypeStruct((M, Np), _F32),
        compiler_params=pltpu.CompilerParams(
            dimension_semantics=("parallel", "parallel"),
            vmem_limit_bytes=V7X_VMEM_LIMIT_BYTES),
        name="proj_matmul",
    )(x, wb)
    return out[:, :N] if n_pad else out


def _mm3(x, w):
    B, T, K = x.shape
    return _mm(x.reshape(B * T, K), w).reshape(B, T, w.shape[1])


def _causal_pairs(S, tq, tk):
    qi, ki = [], []
    for q in range(S // tq):
        for k in range((q * tq + tq - 1) // tk + 1):
            qi.append(q)
            ki.append(k)
    return jnp.asarray(qi, jnp.int32), jnp.asarray(ki, jnp.int32)


def _online_softmax_step(s, v, m_sc, l_sc, acc_sc, h):
    m_old = m_sc[h]
    m_new = jnp.maximum(m_old, jnp.max(s, axis=-1, keepdims=True))
    alpha = jnp.exp(m_old - m_new)
    p = jnp.exp(s - m_new)
    l_sc[h] = alpha * l_sc[h] + jnp.sum(p, axis=-1, keepdims=True)
    acc_sc[h] = alpha * acc_sc[h] + jnp.dot(p.astype(_BF16), v, preferred_element_type=_F32)
    m_sc[h] = m_new


def _fox_attn_kernel(qi_tab, ki_tab, q_ref, k_ref, v_ref, cq_ref, ck_ref, o_ref, m_sc, l_sc, acc_sc,
                     *, tq, tk, nh):
    p = pl.program_id(1)
    qi = qi_tab[p]
    ki = ki_tab[p]
    k_last = (qi * tq + tq - 1) // tk

    @pl.when(ki == 0)
    def _():
        m_sc[...] = jnp.full(m_sc.shape, NEG_BIG, _F32)
        l_sc[...] = jnp.zeros(l_sc.shape, _F32)
        acc_sc[...] = jnp.zeros(acc_sc.shape, _F32)

    rows = qi * tq + lax.broadcasted_iota(jnp.int32, (tq, tk), 0)
    cols = ki * tk + lax.broadcasted_iota(jnp.int32, (tq, tk), 1)
    causal = cols <= rows
    for h in range(nh):
        s = lax.dot_general(q_ref[0, h], k_ref[0, h], (((1,), (1,)), ((), ())),
                            preferred_element_type=_F32)
        s = s + cq_ref[0, h] - ck_ref[0, h]
        s = jnp.where(causal, s, NEG_BIG)
        _online_softmax_step(s, v_ref[0, h], m_sc, l_sc, acc_sc, h)

    @pl.when(ki == k_last)
    def _():
        for h in range(nh):
            o_ref[0, h] = acc_sc[h] / l_sc[h]


def _fox_prompt(q, k, v, logf, tq=512, tk=1024):
    B, S, H, dh = q.shape
    tq, tk = min(tq, S), min(tk, S)
    c = jnp.cumsum(logf, axis=1)
    ct = jnp.swapaxes(c, 1, 2)
    cq = ct[..., None]
    ck = ct[:, :, None, :]
    qh = jnp.swapaxes(q * (HEAD_DIM ** -0.5), 1, 2).astype(_BF16)
    kh = jnp.swapaxes(k, 1, 2).astype(_BF16)
    vh = jnp.swapaxes(v, 1, 2).astype(_BF16)
    qi_tab, ki_tab = _causal_pairs(S, tq, tk)
    grid_spec = pltpu.PrefetchScalarGridSpec(
        num_scalar_prefetch=2,
        grid=(B, int(qi_tab.shape[0])),
        in_specs=[
            pl.BlockSpec((1, H, tq, dh), lambda b, p, qt, kt: (b, 0, qt[p], 0)),
            pl.BlockSpec((1, H, tk, dh), lambda b, p, qt, kt: (b, 0, kt[p], 0)),
            pl.BlockSpec((1, H, tk, dh), lambda b, p, qt, kt: (b, 0, kt[p], 0)),
            pl.BlockSpec((1, H, tq, 1), lambda b, p, qt, kt: (b, 0, qt[p], 0)),
            pl.BlockSpec((1, H, 1, tk), lambda b, p, qt, kt: (b, 0, 0, kt[p])),
        ],
        out_specs=pl.BlockSpec((1, H, tq, dh), lambda b, p, qt, kt: (b, 0, qt[p], 0)),
        scratch_shapes=[pltpu.VMEM((H, tq, 1), _F32), pltpu.VMEM((H, tq, 1), _F32),
                        pltpu.VMEM((H, tq, dh), _F32)],
    )
    out = pl.pallas_call(
        functools.partial(_fox_attn_kernel, tq=tq, tk=tk, nh=H),
        grid_spec=grid_spec,
        out_shape=jax.ShapeDtypeStruct((B, H, S, dh), _F32),
        compiler_params=pltpu.CompilerParams(
            dimension_semantics=("parallel", "arbitrary"),
            vmem_limit_bytes=V7X_VMEM_LIMIT_BYTES),
        name="fox_prompt_attn",
    )(qi_tab, ki_tab, qh, kh, vh, cq, ck)
    return jnp.swapaxes(out, 1, 2).reshape(B, S, H * dh)


def _sortable_key(x):
    b = pltpu.bitcast(x, jnp.int32)
    return b ^ ((b >> 31) & jnp.int32(0x7FFFFFFF))


def _dsa_select_kernel(iq_ref, iw_ref, ikt_ref, mask_ref, key_sc, *, tq, tk, nk, nh, topk):
    qi = pl.program_id(1)
    nkc = (qi * tq + tq - 1) // tk + 1
    rows = qi * tq + lax.broadcasted_iota(jnp.int32, (tq, tk), 0)
    col0 = lax.broadcasted_iota(jnp.int32, (tq, tk), 1)
    iw = iw_ref[0]

    def score_chunk(c, carry):
        kt = ikt_ref[0, c]
        acc = jnp.zeros((tq, tk), _F32)
        for h in range(nh):
            d = jnp.dot(iq_ref[0, h], kt, preferred_element_type=_F32)
            acc = acc + jnp.maximum(d, 0.0) * iw[:, h:h + 1]
        key = _sortable_key(acc)
        key_sc[c] = jnp.where(c * tk + col0 <= rows, key, jnp.int32(INT_MIN))
        return carry

    lax.fori_loop(0, nkc, score_chunk, 0)

    def count(pred):
        def body(c, cnt):
            return cnt + jnp.sum(jnp.where(pred(key_sc[c], c), 1.0, 0.0), axis=-1, keepdims=True)
        return lax.fori_loop(0, nkc, body, jnp.zeros((tq, 1), _F32))

    kf = float(topk)

    def bit_step(i, v):
        cand = v + jnp.left_shift(jnp.int32(1), 31 - i)
        cnt = count(lambda kk, c: kk >= cand)
        return jnp.where(cnt >= kf, cand, v)

    v = lax.fori_loop(0, 32, bit_step, jnp.full((tq, 1), INT_MIN, jnp.int32))
    n_gt = count(lambda kk, c: kk > v)
    n_ge = count(lambda kk, c: kk >= v)
    need = kf - n_gt
    has_thr = v != jnp.int32(INT_MIN)
    ties = jnp.max(jnp.where(has_thr & (n_ge > kf), 1.0, 0.0)) > 0.0

    def tie_search(_):
        def idx_step(i, m):
            cand = m + jnp.left_shift(jnp.int32(1), 30 - i)
            cnt = count(lambda kk, c: (kk == v) & (c * tk + col0 < cand))
            return jnp.where(cnt < need, cand, m)
        return lax.fori_loop(0, 31, idx_step, jnp.zeros((tq, 1), jnp.int32))

    m_last = lax.cond(ties, tie_search, lambda _: jnp.full((tq, 1), 2 ** 31 - 1, jnp.int32), 0)

    def write_chunk(c, carry):
        kk = key_sc[c]
        sel = (kk > v) | ((kk == v) & (c * tk + col0 <= m_last))
        sel = sel & (kk != jnp.int32(INT_MIN))
        mask_ref[0, 0, c] = sel.astype(jnp.int8)
        return carry

    lax.fori_loop(0, nkc, write_chunk, 0)

    def zero_chunk(c, carry):
        mask_ref[0, 0, c] = jnp.zeros((tq, tk), jnp.int8)
        return carry

    lax.fori_loop(nkc, nk, zero_chunk, 0)


def _dsa_select(iq, ik, iw, topk, tq=256, tk=1024):
    B, S, Hi, Di = iq.shape
    tq, tk = min(tq, S), min(tk, S)
    nq, nk = S // tq, S // tk
    iqh = jnp.swapaxes(iq, 1, 2).astype(_BF16)
    ikt = jnp.swapaxes(ik.reshape(B, nk, tk, Di), 2, 3).astype(_BF16)
    iws = (iw * (D_IDX ** -0.5)).astype(_F32)
    return pl.pallas_call(
        functools.partial(_dsa_select_kernel, tq=tq, tk=tk, nk=nk, nh=Hi, topk=topk),
        grid=(B, nq),
        in_specs=[pl.BlockSpec((1, Hi, tq, Di), lambda b, q: (b, 0, q, 0)),
                  pl.BlockSpec((1, tq, Hi), lambda b, q: (b, q, 0)),
                  pl.BlockSpec((1, nk, Di, tk), lambda b, q: (b, 0, 0, 0))],
        out_specs=pl.BlockSpec((1, 1, nk, tq, tk), lambda b, q: (b, q, 0, 0, 0)),
        out_shape=jax.ShapeDtypeStruct((B, nq, nk, tq, tk), jnp.int8),
        scratch_shapes=[pltpu.VMEM((nk, tq, tk), jnp.int32)],
        compiler_params=pltpu.CompilerParams(
            dimension_semantics=("parallel", "arbitrary"),
            vmem_limit_bytes=V7X_VMEM_LIMIT_BYTES),
        name="dsa_topk_select",
    )(iqh, iws, ikt)


def _dsa_attn_kernel(qi_tab, ki_tab, q_ref, k_ref, v_ref, mask_ref, o_ref, m_sc, l_sc, acc_sc,
                     *, tq, tk, nh):
    p = pl.program_id(1)
    qi = qi_tab[p]
    ki = ki_tab[p]
    k_last = (qi * tq + tq - 1) // tk

    @pl.when(ki == 0)
    def _():
        m_sc[...] = jnp.full(m_sc.shape, NEG_BIG, _F32)
        l_sc[...] = jnp.zeros(l_sc.shape, _F32)
        acc_sc[...] = jnp.zeros(acc_sc.shape, _F32)

    keep = mask_ref[0, :, 0].reshape(tq, tk) != 0
    for h in range(nh):
        s = lax.dot_general(q_ref[0, h], k_ref[0, h], (((1,), (1,)), ((), ())),
                            preferred_element_type=_F32)
        s = jnp.where(keep, s, NEG_BIG)
        _online_softmax_step(s, v_ref[0, h], m_sc, l_sc, acc_sc, h)

    @pl.when(ki == k_last)
    def _():
        for h in range(nh):
            o_ref[0, h] = acc_sc[h] / l_sc[h]


def _dsa_prompt(q, k, v, iq, ik, iw, tq=512, tk=1024, tq_sel=256):
    B, S, H, dh = q.shape
    topk = min(TOPK_MAX, S // 4)
    tq, tk, tq_sel = min(tq, S), min(tk, S), min(tq_sel, S)
    mask = _dsa_select(iq, ik, iw, topk, tq=tq_sel, tk=tk)
    r = tq // tq_sel
    qh = jnp.swapaxes(q * (HEAD_DIM ** -0.5), 1, 2).astype(_BF16)
    kh = jnp.swapaxes(k, 1, 2).astype(_BF16)
    vh = jnp.swapaxes(v, 1, 2).astype(_BF16)
    qi_tab, ki_tab = _causal_pairs(S, tq, tk)
    grid_spec = pltpu.PrefetchScalarGridSpec(
        num_scalar_prefetch=2,
        grid=(B, int(qi_tab.shape[0])),
        in_specs=[
            pl.BlockSpec((1, H, tq, dh), lambda b, p, qt, kt: (b, 0, qt[p], 0)),
            pl.BlockSpec((1, H, tk, dh), lambda b, p, qt, kt: (b, 0, kt[p], 0)),
            pl.BlockSpec((1, H, tk, dh), lambda b, p, qt, kt: (b, 0, kt[p], 0)),
            pl.BlockSpec((1, r, 1, tq_sel, tk), lambda b, p, qt, kt: (b, qt[p], kt[p], 0, 0)),
        ],
        out_specs=pl.BlockSpec((1, H, tq, dh), lambda b, p, qt, kt: (b, 0, qt[p], 0)),
        scratch_shapes=[pltpu.VMEM((H, tq, 1), _F32), pltpu.VMEM((H, tq, 1), _F32),
                        pltpu.VMEM((H, tq, dh), _F32)],
    )
    out = pl.pallas_call(
        functools.partial(_dsa_attn_kernel, tq=tq, tk=tk, nh=H),
        grid_spec=grid_spec,
        out_shape=jax.ShapeDtypeStruct((B, H, S, dh), _F32),
        compiler_params=pltpu.CompilerParams(
            dimension_semantics=("parallel", "arbitrary"),
            vmem_limit_bytes=V7X_VMEM_LIMIT_BYTES),
        name="dsa_prompt_attn",
    )(qi_tab, ki_tab, qh, kh, vh, mask)
    return jnp.swapaxes(out, 1, 2).reshape(B, S, H * dh)


def _wkv_kernel(r_ref, w_ref, k_ref, v_ref, a_ref, b_ref, s0_ref, y_ref, sT_ref, s_sc, *, tc, hg, n):
    t_blk = pl.program_id(2)

    @pl.when(t_blk == 0)
    def _():
        s_sc[...] = s0_ref[0]

    eye = (lax.broadcasted_iota(jnp.int32, (n, n), 0)
           == lax.broadcasted_iota(jnp.int32, (n, n), 1)).astype(_F32)

    def step(t, carry):
        for h in range(hg):
            s = s_sc[h]
            a_row = a_ref[0, h, pl.ds(t, 1), :]
            b_row = b_ref[0, h, pl.ds(t, 1), :]
            w_row = w_ref[0, h, pl.ds(t, 1), :]
            k_row = k_ref[0, h, pl.ds(t, 1), :]
            r_row = r_ref[0, h, pl.ds(t, 1), :]
            v_row = v_ref[0, h, pl.ds(t, 1), :]
            sa = jnp.sum(s * a_row, axis=-1, keepdims=True)
            v_col = jnp.sum(eye * v_row, axis=-1, keepdims=True)
            s = s * w_row + sa * b_row + v_col * k_row
            s_sc[h] = s
            y_col = jnp.sum(s * r_row, axis=-1, keepdims=True)
            y_ref[0, h, pl.ds(t, 1), :] = jnp.sum(eye * y_col, axis=0, keepdims=True)
        return carry

    lax.fori_loop(0, tc, step, 0)

    @pl.when(t_blk == pl.num_programs(2) - 1)
    def _():
        sT_ref[0] = s_sc[...]


def _wkv7(s0, r, w, k, v, kk, a, hg=8, tc=256):
    B, T, H, N = r.shape
    tc, hg = min(tc, T), min(hg, H)
    tr = lambda x: jnp.swapaxes(x, 1, 2)
    seq_spec = pl.BlockSpec((1, hg, tc, N), lambda b, g, t: (b, g, t, 0))
    st_spec = pl.BlockSpec((1, hg, N, N), lambda b, g, t: (b, g, 0, 0))
    y, sT = pl.pallas_call(
        functools.partial(_wkv_kernel, tc=tc, hg=hg, n=N),
        grid=(B, H // hg, T // tc),
        in_specs=[seq_spec] * 6 + [st_spec],
        out_specs=[seq_spec, st_spec],
        out_shape=[jax.ShapeDtypeStruct((B, H, T, N), _F32), jax.ShapeDtypeStruct((B, H, N, N), _F32)],
        scratch_shapes=[pltpu.VMEM((hg, N, N), _F32)],
        compiler_params=pltpu.CompilerParams(
            dimension_semantics=("parallel", "parallel", "arbitrary"),
            vmem_limit_bytes=V7X_VMEM_LIMIT_BYTES),
        name="wkv7_scan",
    )(tr(r), tr(w), tr(k), tr(v), tr(-kk), tr(kk * a), s0)
    return sT, tr(y)


def _rmsnorm(x, g):
    y = x * lax.rsqrt(jnp.mean(x * x, axis=-1, keepdims=True) + RMS_EPS)
    return y * g


def _partial_rope(x, pos):
    half = ROPE_DIM // 2
    inv = ROPE_THETA ** (-2.0 * jnp.arange(half, dtype=_F32) / ROPE_DIM)
    ang = pos.astype(_F32)[:, None] * inv
    cos, sin = jnp.cos(ang)[:, None, :], jnp.sin(ang)[:, None, :]
    x1, x2 = x[..., :half], x[..., half:ROPE_DIM]
    return jnp.concatenate([x1 * cos - x2 * sin, x2 * cos + x1 * sin, x[..., ROPE_DIM:]], axis=-1)


def _even_project(hn, w_in, b_forget, pos):
    B, T, _ = hn.shape
    fq, fk, fv, ff, fg, dq, dk, dv, dg, iq, ik, iw = jnp.split(_mm3(hn, w_in), EVEN_CUTS, axis=-1)

    def heads(t, h):
        return t.reshape(B, T, h, -1)

    logf = jax.nn.log_sigmoid(ff + b_forget)
    dq = _partial_rope(heads(dq, H_DSA), pos)
    dk = _partial_rope(heads(dk, H_DSA), pos)
    iq = _partial_rope(heads(iq, H_IDX), pos)
    ik = _partial_rope(ik[:, :, None, :], pos)[:, :, 0]
    iw = iw * (H_IDX ** -0.5)
    return (heads(fq, H_FOX), heads(fk, H_FOX), heads(fv, H_FOX), logf, jax.nn.silu(fg),
            dq, dk, heads(dv, H_DSA), jax.nn.silu(dg), iq, ik, iw)


def _gather_pages(cache, page_table, layer):
    g = cache[page_table, layer]
    return g.reshape((g.shape[0], g.shape[1] * g.shape[2]) + g.shape[3:])


def _fox_sample(q, k, v, logf, k_past, v_past, logf_past):
    T = q.shape[1]
    P = k_past.shape[1]
    scale = HEAD_DIM ** -0.5
    c_past = jnp.cumsum(logf_past.astype(_F32), axis=1)
    c_new = c_past[:, -1:] + jnp.cumsum(logf, axis=1)
    cq = jnp.swapaxes(c_new, 1, 2)[..., None]
    s_p = jnp.einsum('bqhd,bshd->bhqs', q, k_past, preferred_element_type=_F32) * scale
    s_p = s_p + cq - jnp.swapaxes(c_past, 1, 2)[:, :, None, :]
    s_n = jnp.einsum('bqhd,bshd->bhqs', q, k, preferred_element_type=_F32) * scale
    s_n = s_n + cq - jnp.swapaxes(c_new, 1, 2)[:, :, None, :]
    s_n = jnp.where(jnp.arange(T)[None, :] <= jnp.arange(T)[:, None], s_n, -jnp.inf)
    p = jax.nn.softmax(jnp.concatenate([s_p, s_n], axis=-1), axis=-1)
    return (jnp.einsum('bhqs,bshd->bqhd', p[..., :P], v_past)
            + jnp.einsum('bhqs,bshd->bqhd', p[..., P:], v))


def _indexer_scores(qidx, widx, kidx):
    dots = jax.nn.relu(jnp.einsum('bqhd,bld->bqhl', qidx, kidx, preferred_element_type=_F32))
    return jnp.einsum('bqhl,bqh->bql', dots, widx.astype(_F32)) * (D_IDX ** -0.5)


def _dsa_sample(q, k, v, qidx, kidx, widx, cache_k, cache_v, cache_kidx, page_table, layer):
    DB, T = q.shape[:2]
    P = page_table.shape[1] * PAGE_SIZE
    topk = min(TOPK_MAX, (P + T) // 4)
    kidx_all = jnp.concatenate([_gather_pages(cache_kidx, page_table, layer), kidx], axis=1)
    qpos = P + jnp.arange(T)
    kpos = jnp.arange(P + T)
    sc = _indexer_scores(qidx, widx, kidx_all)
    sc = jnp.where((kpos[None, :] <= qpos[:, None])[None], sc, -jnp.inf)
    _, sel = lax.top_k(sc, topk)
    valid = sel <= qpos[None, :, None]
    from_past = valid & (sel < P)
    sp = jnp.minimum(sel, P - 1)
    phys = jnp.take_along_axis(page_table, (sp // PAGE_SIZE).reshape(DB, -1), axis=1).reshape(sel.shape)
    off = sp % PAGE_SIZE
    k_sel = cache_k[phys, layer, off]
    v_sel = cache_v[phys, layer, off]
    new_pos = P + jnp.arange(T)
    from_new = jnp.any(sel[..., None] == new_pos, axis=-2) & (new_pos[None, :] <= qpos[:, None])[None]
    scale = HEAD_DIM ** -0.5
    s_p = jnp.einsum('bqhd,bqkhd->bhqk', q, k_sel, preferred_element_type=_F32) * scale
    s_n = jnp.einsum('bqhd,bshd->bhqs', q, k, preferred_element_type=_F32) * scale
    s = jnp.concatenate([jnp.where(from_past[:, None], s_p, -jnp.inf),
                         jnp.where(from_new[:, None], s_n, -jnp.inf)], axis=-1)
    p = jax.nn.softmax(s, axis=-1)
    return (jnp.einsum('bhqk,bqkhd->bqhd', p[..., :topk], v_sel)
            + jnp.einsum('bhqs,bshd->bqhd', p[..., topk:], v))


def _rwkv_mix(xn, x_prev, s0, v_first, vres, mu, w_rkvg, w_o, w_d0, w_d1, w_d2, w_a0, w_a1, w_a2,
              k_k, k_a, r_k, ln_w, ln_b):
    B, T, D = xn.shape
    H = D // HEAD_DIM
    dx = jnp.concatenate([x_prev[:, None, :], xn[:, :-1]], axis=1) - xn
    xr, xw, xk, xv, xa, xg = [xn + dx * mu[j] for j in range(6)]
    r = _mm3(xr, w_rkvg[0])
    k = _mm3(xk, w_rkvg[1])
    v = _mm3(xv, w_rkvg[2])
    gate = jax.nn.silu(_mm3(xg, w_rkvg[3]))
    w_log = -jax.nn.softplus(-(w_d0 + _mm3(jnp.tanh(_mm3(xw, w_d1)), w_d2))) - 0.5
    decay = jnp.exp(-jnp.exp(w_log))
    a = jax.nn.sigmoid(w_a0 + _mm3(_mm3(xa, w_a1), w_a2))
    if vres is None:
        v_first = v
    else:
        v0, v1, v2 = vres
        v = v + (v_first - v) * jax.nn.sigmoid(v0 + _mm3(_mm3(xv, v1), v2))

    def heads(t):
        return t.reshape(B, T, H, HEAD_DIM)

    r_h, k_h, v_h, a_h, w_h = heads(r), heads(k), heads(v), heads(a), heads(decay)
    kk = heads(k * k_k)
    kk = kk * lax.rsqrt(jnp.maximum(jnp.sum(kk * kk, axis=-1, keepdims=True), 1e-24))
    k_h = k_h * (1.0 + (a_h - 1.0) * k_a.reshape(H, HEAD_DIM))
    s_new, y = _wkv7(s0, r_h, w_h, k_h, v_h, kk, a_h)
    mean = jnp.mean(y, axis=-1, keepdims=True)
    var = jnp.mean(jnp.square(y - mean), axis=-1, keepdims=True)
    y = ((y - mean) * lax.rsqrt(var + GN_EPS) * ln_w.reshape(H, HEAD_DIM) + ln_b.reshape(H, HEAD_DIM))
    y = y + jnp.sum(r_h * k_h * r_k, axis=-1, keepdims=True) * v_h
    out = _mm3(y.reshape(B, T, D) * gate, w_o)
    return out, s_new, xn[:, -1], v_first


def _run_group(x, ple, ctx, prm):
    B, T, D = x.shape
    depth = prm['g_mix'].shape[0]
    past = 0 if ctx is None else ctx['page_table'].shape[1] * PAGE_SIZE
    pos = past + jnp.arange(T, dtype=jnp.int32)
    h = x
    v_first = None
    even_rows, odd_rows = [], []
    for i in range(depth):
        hn = _rmsnorm(h, prm['g_mix'][i])
        if i % 2 == 0:
            e = i // 2
            fq, fk, fv, logf, fg, dq, dk, dv, dg, iq, ik, iw = _even_project(
                hn, prm['w_in'][e], prm['b_forget'][e], pos)
            if ctx is None:
                o_f = _fox_prompt(fq, fk, fv, logf)
                o_d = _dsa_prompt(dq, dk, dv, iq, ik, iw)
            else:
                pt = ctx['page_table']
                o_f = _fox_sample(fq, fk, fv, logf, _gather_pages(ctx['cache_k_fox'], pt, e),
                                  _gather_pages(ctx['cache_v_fox'], pt, e),
                                  _gather_pages(ctx['cache_logf_fox'], pt, e)).reshape(B, T, W_FOX)
                o_d = _dsa_sample(dq, dk, dv, iq, ik, iw, ctx['cache_k_dsa'], ctx['cache_v_dsa'],
                                  ctx['cache_kidx_dsa'], pt, e).reshape(B, T, W_DSA)
            mixed = jnp.concatenate([o_f * fg, o_d * dg], axis=-1)
            h = h + _mm3(mixed, prm['w_out'][e])
            even_rows.append((fk, fv, logf, dk, dv, ik))
        else:
            o = i // 2
            if ctx is None:
                s0 = jnp.zeros((B, D // HEAD_DIM, HEAD_DIM, HEAD_DIM), x.dtype)
                x_prev = jnp.zeros((B, D), x.dtype)
            else:
                s0, x_prev = ctx['state_wkv'][o], ctx['state_shift'][o]
            vres = None if o == 0 else (prm['w_v0'][o - 1], prm['w_v1'][o - 1], prm['w_v2'][o - 1])
            out, s_new, shift_new, v_first = _rwkv_mix(
                hn, x_prev, s0, v_first, vres, prm['mu_rwkv'][o], prm['w_rkvg'][o], prm['w_o_rwkv'][o],
                prm['w_decay0'][o], prm['w_decay1'][o], prm['w_decay2'][o], prm['w_a0'][o], prm['w_a1'][o],
                prm['w_a2'][o], prm['k_k'][o], prm['k_a'][o], prm['r_k'][o], prm['ln_x_w'][o],
                prm['ln_x_b'][o])
            h = h + out
            odd_rows.append((s_new, shift_new))
        gate = jax.nn.sigmoid(_mm3(_rmsnorm(h, prm['g_ple'][i]), prm['w_ple_gate'][i]))
        h = h + gate * _mm3(ple[i], prm['w_ple_proj'][i])
    y = _rmsnorm(h, prm['g_final'])
    new = [jnp.stack([r[j] for r in even_rows], axis=1) for j in range(6)]
    new += [jnp.stack([r[j] for r in odd_rows], axis=0) for j in range(2)]
    return y, new


def kernel(x_prompt, x_sample, cache_k_fox, cache_v_fox, cache_logf_fox, cache_k_dsa, cache_v_dsa, cache_kidx_dsa, state_wkv, state_shift, page_table, p_prompt, p_sample, g_mix, w_in, b_forget, w_out, mu_rwkv, w_rkvg, w_o_rwkv, w_decay0, w_decay1, w_decay2, w_a0, w_a1, w_a2, w_v0, w_v1, w_v2, k_k, k_a, r_k, ln_x_w, ln_x_b, g_ple, w_ple_gate, w_ple_proj, g_final):
    prm = {'g_mix': g_mix, 'w_in': w_in, 'b_forget': b_forget, 'w_out': w_out, 'mu_rwkv': mu_rwkv,
           'w_rkvg': w_rkvg, 'w_o_rwkv': w_o_rwkv, 'w_decay0': w_decay0, 'w_decay1': w_decay1,
           'w_decay2': w_decay2, 'w_a0': w_a0, 'w_a1': w_a1, 'w_a2': w_a2, 'w_v0': w_v0, 'w_v1': w_v1,
           'w_v2': w_v2, 'k_k': k_k, 'k_a': k_a, 'r_k': r_k, 'ln_x_w': ln_x_w, 'ln_x_b': ln_x_b,
           'g_ple': g_ple, 'w_ple_gate': w_ple_gate, 'w_ple_proj': w_ple_proj, 'g_final': g_final}
    y_prompt, new_p = _run_group(x_prompt, p_prompt, None, prm)
    ctx = {'page_table': page_table, 'cache_k_fox': cache_k_fox, 'cache_v_fox': cache_v_fox,
           'cache_logf_fox': cache_logf_fox, 'cache_k_dsa': cache_k_dsa, 'cache_v_dsa': cache_v_dsa,
           'cache_kidx_dsa': cache_kidx_dsa, 'state_wkv': state_wkv, 'state_shift': state_shift}
    y_sample, new_s = _run_group(x_sample, p_sample, ctx, prm)
    return (y_prompt, y_sample) + tuple(new_p) + tuple(new_s)
```

```python
import functools

import jax
import jax.numpy as jnp
import numpy as np
from jax import lax
from jax.experimental import pallas as pl
from jax.experimental.pallas import tpu as pltpu

HEAD_DIM = 64
H_FOX = 8
H_DSA = 8
H_IDX = 8
D_IDX = 64
W_FOX = H_FOX * HEAD_DIM
W_DSA = H_DSA * HEAD_DIM
TOPK_MAX = 256
ROPE_THETA = 500000.0
ROPE_DIM = HEAD_DIM // 4
PAGE_SIZE = 128
RMS_EPS = 1e-6
GN_EPS = 64e-5
EVEN_SIZES = (W_FOX, W_FOX, W_FOX, H_FOX, W_FOX, W_DSA, W_DSA, W_DSA, W_DSA, H_IDX * D_IDX, D_IDX, H_IDX)
EVEN_CUTS = tuple(int(c) for c in np.cumsum(EVEN_SIZES)[:-1])

V7X_LANES = 128
V7X_VMEM_LIMIT_BYTES = 56 * 1024 * 1024
NEG_BIG = -1e30
INT_MIN = -2 ** 31

_BF16 = jnp.bfloat16
_F32 = jnp.float32


def _dot(a, b):
    return jnp.dot(a, b, preferred_element_type=_F32)


def _dot_nt(a, b):
    return lax.dot_general(a, b, (((1,), (1,)), ((), ())), preferred_element_type=_F32)


def _dot_tn(a, b):
    return lax.dot_general(a, b, (((0,), (0,)), ((), ())), preferred_element_type=_F32)


def _mm_kernel(x_ref, w_ref, o_ref):
    o_ref[...] = _dot(x_ref[...].astype(_BF16), w_ref[...])


def _even_proj_kernel(x_ref, w_ref, ws_ref, *o_refs, seg):
    xb = x_ref[...].astype(_BF16)
    for j, o_ref in enumerate(o_refs[:-1]):
        o_ref[...] = _dot(xb, w_ref[:, j * seg:(j + 1) * seg])
    o_refs[-1][...] = _dot(xb, ws_ref[...])


def _even_proj(hn, w_in):
    B, T, K = hn.shape
    M = B * T
    cols = jnp.split(w_in.astype(_BF16), EVEN_CUTS, axis=-1)
    wide = [0, 1, 2, 4, 5, 6, 7, 8, 9]
    narrow = [3, 10, 11]
    seg = W_FOX
    w_main = jnp.concatenate([cols[i] for i in wide], axis=-1)
    w_small = jnp.concatenate([cols[i] for i in narrow], axis=-1)
    n_small = w_small.shape[1]
    w_small = jnp.pad(w_small, ((0, 0), (0, V7X_LANES - n_small)))
    tm = min(512, M)
    outs = pl.pallas_call(
        functools.partial(_even_proj_kernel, seg=seg),
        grid=(M // tm,),
        in_specs=[pl.BlockSpec((tm, K), lambda i: (i, 0)),
                  pl.BlockSpec((K, seg * len(wide)), lambda i: (0, 0)),
                  pl.BlockSpec((K, V7X_LANES), lambda i: (0, 0))],
        out_specs=[pl.BlockSpec((tm, seg), lambda i: (i, 0))] * len(wide)
        + [pl.BlockSpec((tm, V7X_LANES), lambda i: (i, 0))],
        out_shape=[jax.ShapeDtypeStruct((M, seg), _F32)] * len(wide)
        + [jax.ShapeDtypeStruct((M, V7X_LANES), _F32)],
        compiler_params=pltpu.CompilerParams(
            dimension_semantics=("parallel",),
            vmem_limit_bytes=V7X_VMEM_LIMIT_BYTES),
        name="even_in_proj",
    )(hn.reshape(M, K), w_main, w_small)
    groups = [None] * len(EVEN_SIZES)
    for i, o in zip(wide, outs[:-1]):
        groups[i] = o.reshape(B, T, seg)
    off = 0
    for i in narrow:
        groups[i] = outs[-1][:, off:off + EVEN_SIZES[i]].reshape(B, T, EVEN_SIZES[i])
        off += EVEN_SIZES[i]
    return groups


def _mm(x, w):
    M, K = x.shape
    N = w.shape[1]
    n_pad = (-N) % V7X_LANES
    wb = w.astype(_BF16)
    if n_pad:
        wb = jnp.pad(wb, ((0, 0), (0, n_pad)))
    Np = N + n_pad
    tm = min(512, M)
    tn = Np
    for cand in (1024, 768, 512, 384, 256, 128):
        if Np % cand == 0:
            tn = cand
            break
    assert M % tm == 0
    out = pl.pallas_call(
        _mm_kernel,
        grid=(M // tm, Np // tn),
        in_specs=[pl.BlockSpec((tm, K), lambda i, j: (i, 0)),
                  pl.BlockSpec((K, tn), lambda i, j: (0, j))],
        out_specs=pl.BlockSpec((tm, tn), lambda i, j: (i, j)),
        out_shape=jax.ShapeDtypeStruct((M, Np), _F32),
        compiler_params=pltpu.CompilerParams(
            dimension_semantics=("parallel", "parallel"),
            vmem_limit_bytes=V7X_VMEM_LIMIT_BYTES),
        name="proj_matmul",
    )(x, wb)
    return out[:, :N] if n_pad else out


def _mm3(x, w):
    B, T, K = x.shape
    return _mm(x.reshape(B * T, K), w).reshape(B, T, w.shape[1])


def _causal_pairs(S, tq, tk):
    qi, ki = [], []
    for q in range(S // tq):
        for k in range((q * tq + tq - 1) // tk + 1):
            qi.append(q)
            ki.append(k)
    return jnp.asarray(qi, jnp.int32), jnp.asarray(ki, jnp.int32)


def _online_softmax_step(s, v, m_sc, l_sc, acc_sc, h):
    m_old = m_sc[h]
    m_new = jnp.maximum(m_old, jnp.max(s, axis=-1, keepdims=True))
    alpha = jnp.exp(m_old - m_new)
    p = jnp.exp(s - m_new)
    l_sc[h] = alpha * l_sc[h] + jnp.sum(p, axis=-1, keepdims=True)
    acc_sc[h] = alpha * acc_sc[h] + jnp.dot(p.astype(_BF16), v, preferred_element_type=_F32)
    m_sc[h] = m_new


def _fox_attn_kernel(qi_tab, ki_tab, q_ref, k_ref, v_ref, cq_ref, ck_ref, o_ref, m_sc, l_sc, acc_sc,
                     *, tq, tk, nh):
    p = pl.program_id(1)
    qi = qi_tab[p]
    ki = ki_tab[p]
    k_last = (qi * tq + tq - 1) // tk

    @pl.when(ki == 0)
    def _():
        m_sc[...] = jnp.full(m_sc.shape, NEG_BIG, _F32)
        l_sc[...] = jnp.zeros(l_sc.shape, _F32)
        acc_sc[...] = jnp.zeros(acc_sc.shape, _F32)

    rows = qi * tq + lax.broadcasted_iota(jnp.int32, (tq, tk), 0)
    cols = ki * tk + lax.broadcasted_iota(jnp.int32, (tq, tk), 1)
    causal = cols <= rows
    for h in range(nh):
        s = lax.dot_general(q_ref[0, h], k_ref[0, h], (((1,), (1,)), ((), ())),
                            preferred_element_type=_F32)
        s = s + cq_ref[0, h] - ck_ref[0, h]
        s = jnp.where(causal, s, NEG_BIG)
        _online_softmax_step(s, v_ref[0, h], m_sc, l_sc, acc_sc, h)

    @pl.when(ki == k_last)
    def _():
        for h in range(nh):
            o_ref[0, h] = acc_sc[h] / l_sc[h]


def _fox_prompt(q, k, v, logf, tq=512, tk=1024):
    B, S, H, dh = q.shape
    tq, tk = min(tq, S), min(tk, S)
    c = jnp.cumsum(logf, axis=1)
    ct = jnp.swapaxes(c, 1, 2)
    cq = ct[..., None]
    ck = ct[:, :, None, :]
    qh = jnp.swapaxes(q * (HEAD_DIM ** -0.5), 1, 2).astype(_BF16)
    kh = jnp.swapaxes(k, 1, 2).astype(_BF16)
    vh = jnp.swapaxes(v, 1, 2).astype(_BF16)
    qi_tab, ki_tab = _causal_pairs(S, tq, tk)
    grid_spec = pltpu.PrefetchScalarGridSpec(
        num_scalar_prefetch=2,
        grid=(B, int(qi_tab.shape[0])),
        in_specs=[
            pl.BlockSpec((1, H, tq, dh), lambda b, p, qt, kt: (b, 0, qt[p], 0)),
            pl.BlockSpec((1, H, tk, dh), lambda b, p, qt, kt: (b, 0, kt[p], 0)),
            pl.BlockSpec((1, H, tk, dh), lambda b, p, qt, kt: (b, 0, kt[p], 0)),
            pl.BlockSpec((1, H, tq, 1), lambda b, p, qt, kt: (b, 0, qt[p], 0)),
            pl.BlockSpec((1, H, 1, tk), lambda b, p, qt, kt: (b, 0, 0, kt[p])),
        ],
        out_specs=pl.BlockSpec((1, H, tq, dh), lambda b, p, qt, kt: (b, 0, qt[p], 0)),
        scratch_shapes=[pltpu.VMEM((H, tq, 1), _F32), pltpu.VMEM((H, tq, 1), _F32),
                        pltpu.VMEM((H, tq, dh), _F32)],
    )
    out = pl.pallas_call(
        functools.partial(_fox_attn_kernel, tq=tq, tk=tk, nh=H),
        grid_spec=grid_spec,
        out_shape=jax.ShapeDtypeStruct((B, H, S, dh), _F32),
        compiler_params=pltpu.CompilerParams(
            dimension_semantics=("parallel", "arbitrary"),
            vmem_limit_bytes=V7X_VMEM_LIMIT_BYTES),
        name="fox_prompt_attn",
    )(qi_tab, ki_tab, qh, kh, vh, cq, ck)
    return jnp.swapaxes(out, 1, 2).reshape(B, S, H * dh)


def _sortable_key(x):
    b = pltpu.bitcast(x, jnp.int32)
    return b ^ ((b >> 31) & jnp.int32(0x7FFFFFFF))


def _topk_rank_threshold(count, nrows, topk):
    kf = float(topk)

    def bit_step(i, v):
        cand = v + jnp.left_shift(jnp.int32(1), 31 - i)
        cnt = count(lambda kk, cols: kk >= cand)
        return jnp.where(cnt >= kf, cand, v)

    v = lax.fori_loop(0, 32, bit_step, jnp.full((nrows, 1), INT_MIN, jnp.int32))
    n_gt = count(lambda kk, cols: kk > v)
    n_ge = count(lambda kk, cols: kk >= v)
    need = kf - n_gt
    has_thr = v != jnp.int32(INT_MIN)
    ties = jnp.max(jnp.where(has_thr & (n_ge > kf), 1.0, 0.0)) > 0.0

    def tie_search(_):
        def idx_step(i, m):
            cand = m + jnp.left_shift(jnp.int32(1), 30 - i)
            cnt = count(lambda kk, cols: (kk == v) & (cols < cand))
            return jnp.where(cnt < need, cand, m)
        return lax.fori_loop(0, 31, idx_step, jnp.zeros((nrows, 1), jnp.int32))

    m_last = lax.cond(ties, tie_search, lambda _: jnp.full((nrows, 1), 2 ** 31 - 1, jnp.int32), 0)
    return v, m_last


def _topk_keep(kk, cols, v, m_last):
    return ((kk > v) | ((kk == v) & (cols <= m_last))) & (kk != jnp.int32(INT_MIN))


def _dsa_select_kernel(iq_ref, iw_ref, ikt_ref, mask_ref, key_sc, *, tq, tk, nk, nh, topk):
    qi = pl.program_id(1)
    nkc = (qi * tq + tq - 1) // tk + 1
    rows = qi * tq + lax.broadcasted_iota(jnp.int32, (tq, tk), 0)
    col0 = lax.broadcasted_iota(jnp.int32, (tq, tk), 1)
    iw = iw_ref[0]

    def score_chunk(c, carry):
        kt = ikt_ref[0, c]
        acc = jnp.zeros((tq, tk), _F32)
        for h in range(nh):
            d = jnp.dot(iq_ref[0, h], kt, preferred_element_type=_F32)
            acc = acc + jnp.maximum(d, 0.0) * iw[:, h:h + 1]
        key = _sortable_key(acc)
        key_sc[c] = jnp.where(c * tk + col0 <= rows, key, jnp.int32(INT_MIN))
        return carry

    lax.fori_loop(0, nkc, score_chunk, 0)

    def count(pred):
        def body(c, cnt):
            hit = pred(key_sc[c], c * tk + col0)
            return cnt + jnp.sum(jnp.where(hit, 1.0, 0.0), axis=-1, keepdims=True)
        return lax.fori_loop(0, nkc, body, jnp.zeros((tq, 1), _F32))

    v, m_last = _topk_rank_threshold(count, tq, topk)

    def write_chunk(c, carry):
        sel = _topk_keep(key_sc[c], c * tk + col0, v, m_last)
        mask_ref[0, 0, c] = sel.astype(jnp.int8)
        return carry

    lax.fori_loop(0, nkc, write_chunk, 0)

    def zero_chunk(c, carry):
        mask_ref[0, 0, c] = jnp.zeros((tq, tk), jnp.int8)
        return carry

    lax.fori_loop(nkc, nk, zero_chunk, 0)


def _dsa_select(iq, ik, iw, topk, tq=256, tk=1024):
    B, S, Hi, Di = iq.shape
    tq, tk = min(tq, S), min(tk, S)
    nq, nk = S // tq, S // tk
    iqh = jnp.swapaxes(iq, 1, 2).astype(_BF16)
    ikt = jnp.swapaxes(ik.reshape(B, nk, tk, Di), 2, 3).astype(_BF16)
    iws = (iw * (D_IDX ** -0.5)).astype(_F32)
    return pl.pallas_call(
        functools.partial(_dsa_select_kernel, tq=tq, tk=tk, nk=nk, nh=Hi, topk=topk),
        grid=(B, nq),
        in_specs=[pl.BlockSpec((1, Hi, tq, Di), lambda b, q: (b, 0, q, 0)),
                  pl.BlockSpec((1, tq, Hi), lambda b, q: (b, q, 0)),
                  pl.BlockSpec((1, nk, Di, tk), lambda b, q: (b, 0, 0, 0))],
        out_specs=pl.BlockSpec((1, 1, nk, tq, tk), lambda b, q: (b, q, 0, 0, 0)),
        out_shape=jax.ShapeDtypeStruct((B, nq, nk, tq, tk), jnp.int8),
        scratch_shapes=[pltpu.VMEM((nk, tq, tk), jnp.int32)],
        compiler_params=pltpu.CompilerParams(
            dimension_semantics=("parallel", "arbitrary"),
            vmem_limit_bytes=V7X_VMEM_LIMIT_BYTES),
        name="dsa_topk_select",
    )(iqh, iws, ikt)


def _dsa_attn_kernel(qi_tab, ki_tab, q_ref, k_ref, v_ref, mask_ref, o_ref, m_sc, l_sc, acc_sc,
                     *, tq, tk, nh):
    p = pl.program_id(1)
    qi = qi_tab[p]
    ki = ki_tab[p]
    k_last = (qi * tq + tq - 1) // tk

    @pl.when(ki == 0)
    def _():
        m_sc[...] = jnp.full(m_sc.shape, NEG_BIG, _F32)
        l_sc[...] = jnp.zeros(l_sc.shape, _F32)
        acc_sc[...] = jnp.zeros(acc_sc.shape, _F32)

    keep = mask_ref[0, :, 0].reshape(tq, tk) != 0
    for h in range(nh):
        s = lax.dot_general(q_ref[0, h], k_ref[0, h], (((1,), (1,)), ((), ())),
                            preferred_element_type=_F32)
        s = jnp.where(keep, s, NEG_BIG)
        _online_softmax_step(s, v_ref[0, h], m_sc, l_sc, acc_sc, h)

    @pl.when(ki == k_last)
    def _():
        for h in range(nh):
            o_ref[0, h] = acc_sc[h] / l_sc[h]


def _dsa_prompt(q, k, v, iq, ik, iw, tq=512, tk=1024, tq_sel=256):
    B, S, H, dh = q.shape
    topk = min(TOPK_MAX, S // 4)
    tq, tk, tq_sel = min(tq, S), min(tk, S), min(tq_sel, S)
    mask = _dsa_select(iq, ik, iw, topk, tq=tq_sel, tk=tk)
    r = tq // tq_sel
    qh = jnp.swapaxes(q * (HEAD_DIM ** -0.5), 1, 2).astype(_BF16)
    kh = jnp.swapaxes(k, 1, 2).astype(_BF16)
    vh = jnp.swapaxes(v, 1, 2).astype(_BF16)
    qi_tab, ki_tab = _causal_pairs(S, tq, tk)
    grid_spec = pltpu.PrefetchScalarGridSpec(
        num_scalar_prefetch=2,
        grid=(B, int(qi_tab.shape[0])),
        in_specs=[
            pl.BlockSpec((1, H, tq, dh), lambda b, p, qt, kt: (b, 0, qt[p], 0)),
            pl.BlockSpec((1, H, tk, dh), lambda b, p, qt, kt: (b, 0, kt[p], 0)),
            pl.BlockSpec((1, H, tk, dh), lambda b, p, qt, kt: (b, 0, kt[p], 0)),
            pl.BlockSpec((1, r, 1, tq_sel, tk), lambda b, p, qt, kt: (b, qt[p], kt[p], 0, 0)),
        ],
        out_specs=pl.BlockSpec((1, H, tq, dh), lambda b, p, qt, kt: (b, 0, qt[p], 0)),
        scratch_shapes=[pltpu.VMEM((H, tq, 1), _F32), pltpu.VMEM((H, tq, 1), _F32),
                        pltpu.VMEM((H, tq, dh), _F32)],
    )
    out = pl.pallas_call(
        functools.partial(_dsa_attn_kernel, tq=tq, tk=tk, nh=H),
        grid_spec=grid_spec,
        out_shape=jax.ShapeDtypeStruct((B, H, S, dh), _F32),
        compiler_params=pltpu.CompilerParams(
            dimension_semantics=("parallel", "arbitrary"),
            vmem_limit_bytes=V7X_VMEM_LIMIT_BYTES),
        name="dsa_prompt_attn",
    )(qi_tab, ki_tab, qh, kh, vh, mask)
    return jnp.swapaxes(out, 1, 2).reshape(B, S, H * dh)


def _split2(x):
    hi = x.astype(_BF16)
    lo = (x - hi.astype(_F32)).astype(_BF16)
    return hi, lo


def _dot3(a, b):
    ah, al = a
    bh, bl = b
    return _dot(ah, bh) + _dot(ah, bl) + _dot(al, bh)


def _wkv_chunk_kernel(r_ref, lw_ref, k_ref, v_ref, al_ref, be_ref, s0_ref, y_ref, sT_ref, s_sc,
                      *, L, n, npair, nchunk):
    t_blk = pl.program_id(2)
    L2, n2 = 2 * L, 2 * n

    @pl.when(t_blk == 0)
    def _():
        s_sc[...] = s0_ref[0]

    ri = lax.broadcasted_iota(jnp.int32, (L2, L2), 0)
    ci = lax.broadcasted_iota(jnp.int32, (L2, L2), 1)
    same_head = (ri // L) == (ci // L)
    strict = same_head & (ci < ri)
    incl = same_head & (ci <= ri)
    bs = min(16, L)
    diag_blk = (ri // bs) == (ci // bs)
    eye = (ri == ci).astype(_F32)
    cum_tri = (lax.broadcasted_iota(jnp.int32, (L, L), 0)
               >= lax.broadcasted_iota(jnp.int32, (L, L), 1)).astype(_BF16)
    sr = lax.broadcasted_iota(jnp.int32, (n2, n2), 0)
    sc = lax.broadcasted_iota(jnp.int32, (n2, n2), 1)
    state_blk = (sr // n) == (sc // n)
    lane_a = lax.broadcasted_iota(jnp.int32, (L, n2), 1) < n

    def expand(x):
        return jnp.concatenate([jnp.where(lane_a, x, 0.0), jnp.where(lane_a, 0.0, x)], axis=0)

    def compact(x):
        return x[:L] + x[L:]

    def each(fn, *cols):
        return [fn(*args) for args in zip(*cols)]

    def chunk(c, carry):
        rows = pl.ds(pl.multiple_of(c * L, L), L)
        lane_sl = [slice(p * n2, (p + 1) * n2) for p in range(npair)]
        lw = [lw_ref[0, rows, ls] for ls in lane_sl]
        r = [r_ref[0, rows, ls] for ls in lane_sl]
        k = [k_ref[0, rows, ls] for ls in lane_sl]
        v = [v_ref[0, rows, ls] for ls in lane_sl]
        al = [al_ref[0, rows, ls] for ls in lane_sl]
        be = [be_ref[0, rows, ls] for ls in lane_sl]

        def cumsum(x):
            hi = x.astype(_BF16)
            r1 = x - hi.astype(_F32)
            mid = r1.astype(_BF16)
            lo = (r1 - mid.astype(_F32)).astype(_BF16)
            return _dot(cum_tri, hi) + _dot(cum_tri, mid) + _dot(cum_tri, lo)

        g = each(cumsum, lw)
        g_last = each(lambda x: x[L - 1:L, :], g)
        a_t = each(lambda x, gg, l: x * jnp.exp(gg - l), al, g, lw)
        r_t = each(lambda x, gg: x * jnp.exp(gg), r, g)
        eng = each(lambda gg: jnp.exp(-gg), g)
        pa_pr = each(lambda x, y: jnp.concatenate([expand(x), expand(y)], axis=0).astype(_BF16), a_t, r_t)
        pb = each(lambda x, e: expand(x * e).astype(_BF16), be, eng)
        pk = each(lambda x, e: expand(x * e).astype(_BF16), k, eng)
        pv = each(lambda x: expand(x).astype(_BF16), v)
        xb = each(_dot_nt, pa_pr, pb)
        xk = each(_dot_nt, pa_pr, pk)
        m_ab = each(lambda x: jnp.where(strict, x[:L2], 0.0), xb)
        m_ak = each(lambda x: jnp.where(strict, x[:L2], 0.0).astype(_BF16), xk)
        n_rb = each(lambda x: jnp.where(incl, x[L2:], 0.0).astype(_BF16), xb)
        n_rk = each(lambda x: jnp.where(incl, x[L2:], 0.0).astype(_BF16), xk)
        d1 = each(lambda x: jnp.where(diag_blk, x, 0.0), m_ab)
        e_s = each(lambda x, d: _split2(x - d), m_ab, d1)
        d1s = each(_split2, d1)
        d2 = each(_dot3, d1s, d1s)
        d2s = each(_split2, d2)
        d4 = each(_dot3, d2s, d2s)
        d4s = each(_split2, d4)
        d8 = each(_dot3, d4s, d4s)
        p12 = each(lambda x, y: _dot3(_split2(eye + x), _split2(eye + y)), d1, d2)
        p48 = each(lambda x, y: _dot3(_split2(eye + x), _split2(eye + y)), d4, d8)
        xs = each(lambda x, y: _split2(_dot3(_split2(x), _split2(y))), p12, p48)
        nn = each(_dot3, xs, e_s)
        nns = each(_split2, nn)
        nn2 = each(_dot3, nns, nns)
        q12 = each(lambda x, y: _dot3(_split2(eye + x), _split2(eye + y)), nn, nn2)
        tinv = each(lambda x, y: _dot3(_split2(x), y).astype(_BF16), q12, xs)
        s = [s_sc[p] for p in range(npair)]
        ars = each(lambda x, y, st: _dot_nt(jnp.concatenate([x, y], axis=0).astype(_BF16), st.astype(_BF16)),
                   a_t, r_t, s)
        rhs = each(lambda x, m, vv: (expand(x[:L]) + _dot(m, vv)).astype(_BF16), ars, m_ak, pv)
        u_e = each(_dot, tinv, rhs)
        y_e = each(lambda x, nb, u, nk, vv: expand(x[L:]) + _dot(nb, u.astype(_BF16)) + _dot(nk, vv),
                   ars, n_rb, u_e, n_rk, pv)
        for ls, ye in zip(lane_sl, y_e):
            y_ref[0, rows, ls] = compact(ye)
        dec = each(lambda gl, gg: jnp.exp(gl - gg), g_last, g)
        uv = each(lambda u, vv: jnp.concatenate([compact(u), vv], axis=0).astype(_BF16), u_e, v)
        bk = each(lambda x, y, d: jnp.concatenate([x * d, y * d], axis=0).astype(_BF16), be, k, dec)
        upd = each(_dot_tn, uv, bk)
        for p in range(npair):
            s_sc[p] = s[p] * jnp.exp(g_last[p]) + jnp.where(state_blk, upd[p], 0.0)
        return carry

    lax.fori_loop(0, nchunk, chunk, 0)

    @pl.when(t_blk == pl.num_programs(2) - 1)
    def _():
        sT_ref[0] = s_sc[...]


def _wkv7(s0, r, lw, k, v, kk, a, L=64, tc=256, npair=8):
    B, T, H, N = r.shape
    L, tc = min(L, T), min(tc, T)
    npair = min(npair, H // 2)
    D = H * N
    flat = lambda x: x.reshape(B, T, D)
    s0p = s0.reshape(B, H // 2, 2, N, N)
    z = jnp.zeros_like(s0p[:, :, 0])
    s0bd = jnp.concatenate([jnp.concatenate([s0p[:, :, 0], z], axis=-1),
                            jnp.concatenate([z, s0p[:, :, 1]], axis=-1)], axis=-2)
    seq_spec = pl.BlockSpec((1, tc, npair * 2 * N), lambda b, g, t: (b, t, g))
    st_spec = pl.BlockSpec((1, npair, 2 * N, 2 * N), lambda b, g, t: (b, g, 0, 0))
    y, sT = pl.pallas_call(
        functools.partial(_wkv_chunk_kernel, L=L, n=N, npair=npair, nchunk=tc // L),
        grid=(B, H // (2 * npair), T // tc),
        in_specs=[seq_spec] * 6 + [st_spec],
        out_specs=[seq_spec, st_spec],
        out_shape=[jax.ShapeDtypeStruct((B, T, D), _F32),
                   jax.ShapeDtypeStruct((B, H // 2, 2 * N, 2 * N), _F32)],
        scratch_shapes=[pltpu.VMEM((npair, 2 * N, 2 * N), _F32)],
        compiler_params=pltpu.CompilerParams(
            dimension_semantics=("parallel", "parallel", "arbitrary"),
            vmem_limit_bytes=V7X_VMEM_LIMIT_BYTES),
        name="wkv7_chunked",
    )(flat(r), flat(lw), flat(k), flat(v), flat(-kk), flat(kk * a), s0bd)
    sT = jnp.stack([sT[:, :, :N, :N], sT[:, :, N:, N:]], axis=2).reshape(B, H, N, N)
    return sT, y.reshape(B, T, H, N)


def _rmsnorm(x, g):
    y = x * lax.rsqrt(jnp.mean(x * x, axis=-1, keepdims=True) + RMS_EPS)
    return y * g


def _partial_rope(x, pos):
    half = ROPE_DIM // 2
    inv = ROPE_THETA ** (-2.0 * jnp.arange(half, dtype=_F32) / ROPE_DIM)
    ang = pos.astype(_F32)[:, None] * inv
    cos, sin = jnp.cos(ang)[:, None, :], jnp.sin(ang)[:, None, :]
    x1, x2 = x[..., :half], x[..., half:ROPE_DIM]
    return jnp.concatenate([x1 * cos - x2 * sin, x2 * cos + x1 * sin, x[..., ROPE_DIM:]], axis=-1)


def _even_project(hn, w_in, b_forget, pos):
    B, T, _ = hn.shape
    fq, fk, fv, ff, fg, dq, dk, dv, dg, iq, ik, iw = _even_proj(hn, w_in)

    def heads(t, h):
        return t.reshape(B, T, h, -1)

    logf = jax.nn.log_sigmoid(ff + b_forget)
    dq = _partial_rope(heads(dq, H_DSA), pos)
    dk = _partial_rope(heads(dk, H_DSA), pos)
    iq = _partial_rope(heads(iq, H_IDX), pos)
    ik = _partial_rope(ik[:, :, None, :], pos)[:, :, 0]
    iw = iw * (H_IDX ** -0.5)
    return (heads(fq, H_FOX), heads(fk, H_FOX), heads(fv, H_FOX), logf, jax.nn.silu(fg),
            dq, dk, heads(dv, H_DSA), jax.nn.silu(dg), iq, ik, iw)


PAGES_PER_STEP = 8
NEW_ROWS_PAD = 16


def _paged_attn_kernel(pt_ref, q_ref, roff_ref, bias_ref, *refs, pg, page):
    k_refs, v_refs = refs[:pg], refs[pg:2 * pg]
    kn_ref, vn_ref, bn_ref, o_ref, m_sc, l_sc, acc_sc = refs[2 * pg:]
    j = pl.program_id(1)

    @pl.when(j == 0)
    def _():
        m_sc[...] = jnp.full(m_sc.shape, NEG_BIG, _F32)
        l_sc[...] = jnp.zeros(l_sc.shape, _F32)
        acc_sc[...] = jnp.zeros(acc_sc.shape, _F32)

    q = q_ref[0]
    roff = roff_ref[0]
    nrep = q.shape[0] // bias_ref.shape[1]

    def update(scores, values):
        m_old = m_sc[...]
        m_new = m_old
        for s in scores:
            m_new = jnp.maximum(m_new, jnp.max(s, axis=-1, keepdims=True))
        alpha = jnp.exp(m_old - m_new)
        l_new = alpha * l_sc[...]
        acc = alpha * acc_sc[...]
        for s, v in zip(scores, values):
            p = jnp.exp(s - m_new)
            l_new = l_new + jnp.sum(p, axis=-1, keepdims=True)
            acc = acc + _dot(p.astype(_BF16), v)
        l_sc[...] = l_new
        acc_sc[...] = acc
        m_sc[...] = m_new

    scores = []
    for i in range(pg):
        b = bias_ref[0, :, i * page:(i + 1) * page]
        scores.append(_dot_nt(q, k_refs[i][0, 0].astype(_BF16)) + roff + jnp.concatenate([b] * nrep, axis=0))
    update(scores, [v_refs[i][0, 0].astype(_BF16) for i in range(pg)])

    @pl.when(j == pl.num_programs(1) - 1)
    def _():
        update([_dot_nt(q, kn_ref[0].astype(_BF16)) + bn_ref[0]], [vn_ref[0].astype(_BF16)])
        o_ref[0] = acc_sc[...] / l_sc[...]


def _paged_attention(qbd, roff, bias_t, cache_k, cache_v, page_table, layer, k_new, v_new, bias_new):
    B, R, W = qbd.shape
    npages = page_table.shape[1]
    pg = min(PAGES_PER_STEP, npages)
    assert npages % pg == 0
    tn = k_new.shape[1]
    tile_rows = bias_t.shape[1]

    def page_spec(i):
        return pl.BlockSpec((1, 1, PAGE_SIZE, W),
                            lambda b, j, pt, i=i: (pt[b * npages + j * pg + i], layer, 0, 0))

    grid_spec = pltpu.PrefetchScalarGridSpec(
        num_scalar_prefetch=1,
        grid=(B, npages // pg),
        in_specs=[pl.BlockSpec((1, R, W), lambda b, j, pt: (b, 0, 0)),
                  pl.BlockSpec((1, R, 1), lambda b, j, pt: (b, 0, 0)),
                  pl.BlockSpec((1, tile_rows, pg * PAGE_SIZE), lambda b, j, pt: (b, 0, j))]
        + [page_spec(i) for i in range(pg)] + [page_spec(i) for i in range(pg)]
        + [pl.BlockSpec((1, tn, W), lambda b, j, pt: (b, 0, 0)),
           pl.BlockSpec((1, tn, W), lambda b, j, pt: (b, 0, 0)),
           pl.BlockSpec((1, R, tn), lambda b, j, pt: (b, 0, 0))],
        out_specs=pl.BlockSpec((1, R, W), lambda b, j, pt: (b, 0, 0)),
        scratch_shapes=[pltpu.VMEM((R, 1), _F32), pltpu.VMEM((R, 1), _F32), pltpu.VMEM((R, W), _F32)],
    )
    return pl.pallas_call(
        functools.partial(_paged_attn_kernel, pg=pg, page=PAGE_SIZE),
        grid_spec=grid_spec,
        out_shape=jax.ShapeDtypeStruct((B, R, W), _F32),
        compiler_params=pltpu.CompilerParams(
            dimension_semantics=("parallel", "arbitrary"),
            vmem_limit_bytes=V7X_VMEM_LIMIT_BYTES),
        name="paged_decode_attn",
    )(page_table.reshape(-1), qbd, roff, bias_t, *([cache_k] * pg), *([cache_v] * pg), k_new, v_new, bias_new)


def _pad_rows(x, rows, value=0.0):
    return jnp.pad(x, ((0, 0), (0, rows - x.shape[1]), (0, 0)), constant_values=value)


def _fox_sample(q, k, v, logf, cache_k, cache_v, cache_logf, page_table, layer):
    B, T, H, dh = q.shape
    W = H * dh
    n_pool, n_even = cache_k.shape[:2]
    logf_past = cache_logf[page_table, layer].reshape(B, -1, H)
    c_past = jnp.cumsum(logf_past, axis=1)
    c_new = c_past[:, -1:] + jnp.cumsum(logf, axis=1)
    eye = jnp.eye(H, dtype=_F32)
    qs = q * (HEAD_DIM ** -0.5)
    qbd = (qs[:, :, :, None, :] * eye[None, None, :, :, None]).reshape(B, T * H, W).astype(_BF16)
    roff = c_new.reshape(B, T * H, 1)
    bias_t = -jnp.swapaxes(c_past, 1, 2)
    tri = jnp.arange(T)[None, :] <= jnp.arange(T)[:, None]
    bn = c_new[:, :, None, :] - c_new[:, None, :, :]
    bn = jnp.where(tri[None, :, :, None], bn, NEG_BIG)
    bn = jnp.swapaxes(bn, 2, 3).reshape(B, T * H, T)
    bn = jnp.pad(bn, ((0, 0), (0, 0), (0, NEW_ROWS_PAD - T)), constant_values=NEG_BIG)
    out = _paged_attention(qbd, roff, bias_t,
                           cache_k.reshape(n_pool, n_even, PAGE_SIZE, W),
                           cache_v.reshape(n_pool, n_even, PAGE_SIZE, W), page_table, layer,
                           _pad_rows(k.reshape(B, T, W), NEW_ROWS_PAD),
                           _pad_rows(v.reshape(B, T, W), NEW_ROWS_PAD), bn)
    out = out.reshape(B, T, H, H, dh)
    return jnp.stack([out[:, :, h, h] for h in range(H)], axis=2).reshape(B, T, W)


def _dsa_sample_select_kernel(pt_ref, x_ref, w_ref, kn_ref, *refs, npages, page, nq, nh, topk):
    kp_refs = refs[:npages]
    bias_ref, key_sc = refs[npages:]
    i = pl.program_id(1)
    x = x_ref[0]
    w = w_ref[0]
    r0 = pl.multiple_of(i * nq, nq)

    def fold(d):
        t = jnp.maximum(d, 0.0) * w
        acc = t[0:nq]
        for h in range(1, nh):
            acc = acc + t[h * nq:(h + 1) * nq]
        return acc

    for p in range(npages):
        d = _dot_nt(x, kp_refs[p][0, 0].astype(_BF16))
        key_sc[pl.ds(r0, nq), p * page:(p + 1) * page] = _sortable_key(fold(d))
    dn = _dot_nt(x, kn_ref[0].astype(_BF16))
    qrow = lax.broadcasted_iota(jnp.int32, (nq, page), 0)
    col = lax.broadcasted_iota(jnp.int32, (nq, page), 1)
    key_sc[pl.ds(r0, nq), npages * page:(npages + 1) * page] = jnp.where(
        col <= qrow, _sortable_key(fold(dn)), jnp.int32(INT_MIN))

    @pl.when(i == pl.num_programs(1) - 1)
    def _():
        nrows, ncols = key_sc.shape
        cols = lax.broadcasted_iota(jnp.int32, (nrows, ncols), 1)

        def count(pred):
            return jnp.sum(jnp.where(pred(key_sc[...], cols), 1.0, 0.0), axis=-1, keepdims=True)

        v, m_last = _topk_rank_threshold(count, nrows, topk)
        bias_ref[0] = jnp.where(_topk_keep(key_sc[...], cols, v, m_last), 0.0, NEG_BIG)


def _dsa_sample_select(iq, ik, iw, cache_kidx, page_table, layer, topk, gb=8):
    B, T, Hi, Di = iq.shape
    npages = page_table.shape[1]
    gb = min(gb, B)
    assert B % gb == 0
    ncols = (npages + 1) * PAGE_SIZE
    x = jnp.swapaxes(iq, 1, 2).reshape(B, Hi * T, Di).astype(_BF16)
    w = jnp.swapaxes(iw * (D_IDX ** -0.5), 1, 2).reshape(B, Hi * T, 1)
    kn = _pad_rows(ik, PAGE_SIZE)

    def page_spec(p):
        return pl.BlockSpec((1, 1, PAGE_SIZE, Di),
                            lambda g, i, pt, p=p: (pt[(g * gb + i) * npages + p], layer, 0, 0))

    grid_spec = pltpu.PrefetchScalarGridSpec(
        num_scalar_prefetch=1,
        grid=(B // gb, gb),
        in_specs=[pl.BlockSpec((1, Hi * T, Di), lambda g, i, pt: (g * gb + i, 0, 0)),
                  pl.BlockSpec((1, Hi * T, 1), lambda g, i, pt: (g * gb + i, 0, 0)),
                  pl.BlockSpec((1, PAGE_SIZE, Di), lambda g, i, pt: (g * gb + i, 0, 0))]
        + [page_spec(p) for p in range(npages)],
        out_specs=pl.BlockSpec((1, gb * T, ncols), lambda g, i, pt: (g, 0, 0)),
        scratch_shapes=[pltpu.VMEM((gb * T, ncols), jnp.int32)],
    )
    bias = pl.pallas_call(
        functools.partial(_dsa_sample_select_kernel, npages=npages, page=PAGE_SIZE, nq=T, nh=Hi, topk=topk),
        grid_spec=grid_spec,
        out_shape=jax.ShapeDtypeStruct((B // gb, gb * T, ncols), _F32),
        compiler_params=pltpu.CompilerParams(
            dimension_semantics=("parallel", "arbitrary"),
            vmem_limit_bytes=V7X_VMEM_LIMIT_BYTES),
        name="dsa_sample_select",
    )(page_table.reshape(-1), x, w, kn, *([cache_kidx] * npages))
    return bias.reshape(B, T, ncols)


def _dsa_sample(q, k, v, iq, ik, iw, cache_k, cache_v, cache_kidx, page_table, layer):
    B, T, H, dh = q.shape
    W = H * dh
    n_pool, n_even = cache_k.shape[:2]
    P = page_table.shape[1] * PAGE_SIZE
    topk = min(TOPK_MAX, (P + T) // 4)
    sel = _dsa_sample_select(iq, ik, iw, cache_kidx, page_table, layer, topk)
    eye = jnp.eye(H, dtype=_F32)
    qh = jnp.swapaxes(q * (HEAD_DIM ** -0.5), 1, 2)
    qbd = (qh[:, :, :, None, :] * eye[None, :, None, :, None]).reshape(B, H * T, W).astype(_BF16)
    roff = jnp.zeros((B, H * T, 1), _F32)
    bn = jnp.tile(sel[:, :, P:P + NEW_ROWS_PAD], (1, H, 1))
    out = _paged_attention(qbd, roff, sel[:, :, :P],
                           cache_k.reshape(n_pool, n_even, PAGE_SIZE, W),
                           cache_v.reshape(n_pool, n_even, PAGE_SIZE, W), page_table, layer,
                           _pad_rows(k.reshape(B, T, W), NEW_ROWS_PAD),
                           _pad_rows(v.reshape(B, T, W), NEW_ROWS_PAD), bn)
    out = out.reshape(B, H, T, H, dh)
    return jnp.stack([out[:, h, :, h] for h in range(H)], axis=2).reshape(B, T, W)


def _rwkv_mix(xn, x_prev, s0, v_first, vres, mu, w_rkvg, w_o, w_d0, w_d1, w_d2, w_a0, w_a1, w_a2,
              k_k, k_a, r_k, ln_w, ln_b):
    B, T, D = xn.shape
    H = D // HEAD_DIM
    dx = jnp.concatenate([x_prev[:, None, :], xn[:, :-1]], axis=1) - xn
    xr, xw, xk, xv, xa, xg = [xn + dx * mu[j] for j in range(6)]
    r = _mm3(xr, w_rkvg[0])
    k = _mm3(xk, w_rkvg[1])
    v = _mm3(xv, w_rkvg[2])
    gate = jax.nn.silu(_mm3(xg, w_rkvg[3]))
    w_log = -jax.nn.softplus(-(w_d0 + _mm3(jnp.tanh(_mm3(xw, w_d1)), w_d2))) - 0.5
    log_decay = -jnp.exp(w_log)
    a = jax.nn.sigmoid(w_a0 + _mm3(_mm3(xa, w_a1), w_a2))
    if vres is None:
        v_first = v
    else:
        v0, v1, v2 = vres
        v = v + (v_first - v) * jax.nn.sigmoid(v0 + _mm3(_mm3(xv, v1), v2))

    def heads(t):
        return t.reshape(B, T, H, HEAD_DIM)

    r_h, k_h, v_h, a_h, w_h = heads(r), heads(k), heads(v), heads(a), heads(log_decay)
    kk = heads(k * k_k)
    kk = kk * lax.rsqrt(jnp.maximum(jnp.sum(kk * kk, axis=-1, keepdims=True), 1e-24))
    k_h = k_h * (1.0 + (a_h - 1.0) * k_a.reshape(H, HEAD_DIM))
    s_new, y = _wkv7(s0, r_h, w_h, k_h, v_h, kk, a_h)
    mean = jnp.mean(y, axis=-1, keepdims=True)
    var = jnp.mean(jnp.square(y - mean), axis=-1, keepdims=True)
    y = ((y - mean) * lax.rsqrt(var + GN_EPS) * ln_w.reshape(H, HEAD_DIM) + ln_b.reshape(H, HEAD_DIM))
    y = y + jnp.sum(r_h * k_h * r_k, axis=-1, keepdims=True) * v_h
    out = _mm3(y.reshape(B, T, D) * gate, w_o)
    return out, s_new, xn[:, -1], v_first


def _run_group(x, ple, ctx, prm):
    B, T, D = x.shape
    depth = prm['g_mix'].shape[0]
    past = 0 if ctx is None else ctx['page_table'].shape[1] * PAGE_SIZE
    pos = past + jnp.arange(T, dtype=jnp.int32)
    h = x
    v_first = None
    even_rows, odd_rows = [], []
    for i in range(depth):
        hn = _rmsnorm(h, prm['g_mix'][i])
        if i % 2 == 0:
            e = i // 2
            fq, fk, fv, logf, fg, dq, dk, dv, dg, iq, ik, iw = _even_project(
                hn, prm['w_in'][e], prm['b_forget'][e], pos)
            if ctx is None:
                o_f = _fox_prompt(fq, fk, fv, logf)
                o_d = _dsa_prompt(dq, dk, dv, iq, ik, iw)
            else:
                pt = ctx['page_table']
                o_f = _fox_sample(fq, fk, fv, logf, ctx['cache_k_fox'], ctx['cache_v_fox'],
                                  ctx['cache_logf_fox'], pt, e)
                o_d = _dsa_sample(dq, dk, dv, iq, ik, iw, ctx['cache_k_dsa'], ctx['cache_v_dsa'],
                                  ctx['cache_kidx_dsa'], pt, e)
            mixed = jnp.concatenate([o_f * fg, o_d * dg], axis=-1)
            h = h + _mm3(mixed, prm['w_out'][e])
            even_rows.append((fk, fv, logf, dk, dv, ik))
        else:
            o = i // 2
            if ctx is None:
                s0 = jnp.zeros((B, D // HEAD_DIM, HEAD_DIM, HEAD_DIM), x.dtype)
                x_prev = jnp.zeros((B, D), x.dtype)
            else:
                s0, x_prev = ctx['state_wkv'][o], ctx['state_shift'][o]
            vres = None if o == 0 else (prm['w_v0'][o - 1], prm['w_v1'][o - 1], prm['w_v2'][o - 1])
            out, s_new, shift_new, v_first = _rwkv_mix(
                hn, x_prev, s0, v_first, vres, prm['mu_rwkv'][o], prm['w_rkvg'][o], prm['w_o_rwkv'][o],
                prm['w_decay0'][o], prm['w_decay1'][o], prm['w_decay2'][o], prm['w_a0'][o], prm['w_a1'][o],
                prm['w_a2'][o], prm['k_k'][o], prm['k_a'][o], prm['r_k'][o], prm['ln_x_w'][o],
                prm['ln_x_b'][o])
            h = h + out
            odd_rows.append((s_new, shift_new))
        gate = jax.nn.sigmoid(_mm3(_rmsnorm(h, prm['g_ple'][i]), prm['w_ple_gate'][i]))
        h = h + gate * _mm3(ple[i], prm['w_ple_proj'][i])
    y = _rmsnorm(h, prm['g_final'])
    new = [jnp.stack([r[j] for r in even_rows], axis=1) for j in range(6)]
    new += [jnp.stack([r[j] for r in odd_rows], axis=0) for j in range(2)]
    return y, new


def kernel(x_prompt, x_sample, cache_k_fox, cache_v_fox, cache_logf_fox, cache_k_dsa, cache_v_dsa, cache_kidx_dsa, state_wkv, state_shift, page_table, p_prompt, p_sample, g_mix, w_in, b_forget, w_out, mu_rwkv, w_rkvg, w_o_rwkv, w_decay0, w_decay1, w_decay2, w_a0, w_a1, w_a2, w_v0, w_v1, w_v2, k_k, k_a, r_k, ln_x_w, ln_x_b, g_ple, w_ple_gate, w_ple_proj, g_final):
    prm = {'g_mix': g_mix, 'w_in': w_in, 'b_forget': b_forget, 'w_out': w_out, 'mu_rwkv': mu_rwkv,
           'w_rkvg': w_rkvg, 'w_o_rwkv': w_o_rwkv, 'w_decay0': w_decay0, 'w_decay1': w_decay1,
           'w_decay2': w_decay2, 'w_a0': w_a0, 'w_a1': w_a1, 'w_a2': w_a2, 'w_v0': w_v0, 'w_v1': w_v1,
           'w_v2': w_v2, 'k_k': k_k, 'k_a': k_a, 'r_k': r_k, 'ln_x_w': ln_x_w, 'ln_x_b': ln_x_b,
           'g_ple': g_ple, 'w_ple_gate': w_ple_gate, 'w_ple_proj': w_ple_proj, 'g_final': g_final}
    y_prompt, new_p = _run_group(x_prompt, p_prompt, None, prm)
    ctx = {'page_table': page_table, 'cache_k_fox': cache_k_fox, 'cache_v_fox': cache_v_fox,
           'cache_logf_fox': cache_logf_fox, 'cache_k_dsa': cache_k_dsa, 'cache_v_dsa': cache_v_dsa,
           'cache_kidx_dsa': cache_kidx_dsa, 'state_wkv': state_wkv, 'state_shift': state_shift}
    y_sample, new_s = _run_group(x_sample, p_sample, ctx, prm)
    return (y_prompt, y_sample) + tuple(new_p) + tuple(new_s)
```

```python
import functools

import jax
import jax.numpy as jnp
import numpy as np
from jax import lax
from jax.experimental import pallas as pl
from jax.experimental.pallas import tpu as pltpu

HEAD_DIM = 64
H_FOX = 8
H_DSA = 8
H_IDX = 8
D_IDX = 64
W_FOX = H_FOX * HEAD_DIM
W_DSA = H_DSA * HEAD_DIM
TOPK_MAX = 256
ROPE_THETA = 500000.0
ROPE_DIM = HEAD_DIM // 4
PAGE_SIZE = 128
RMS_EPS = 1e-6
GN_EPS = 64e-5
EVEN_SIZES = (W_FOX, W_FOX, W_FOX, H_FOX, W_FOX, W_DSA, W_DSA, W_DSA, W_DSA, H_IDX * D_IDX, D_IDX, H_IDX)
EVEN_CUTS = tuple(int(c) for c in np.cumsum(EVEN_SIZES)[:-1])

V7X_LANES = 128
V7X_VMEM_LIMIT_BYTES = 56 * 1024 * 1024
NEG_BIG = -1e30
INT_MIN = -2 ** 31

_BF16 = jnp.bfloat16
_F32 = jnp.float32


def _dot(a, b):
    return jnp.dot(a, b, preferred_element_type=_F32)


def _dot_nt(a, b):
    return lax.dot_general(a, b, (((1,), (1,)), ((), ())), preferred_element_type=_F32)


def _dot_tn(a, b):
    return lax.dot_general(a, b, (((0,), (0,)), ((), ())), preferred_element_type=_F32)


def _mm_kernel(x_ref, w_ref, o_ref):
    o_ref[...] = _dot(x_ref[...].astype(_BF16), w_ref[...])


def _even_proj_kernel(x_ref, w_ref, ws_ref, *o_refs, seg):
    xb = x_ref[...].astype(_BF16)
    for j, o_ref in enumerate(o_refs[:-1]):
        o_ref[...] = _dot(xb, w_ref[:, j * seg:(j + 1) * seg])
    o_refs[-1][...] = _dot(xb, ws_ref[...])


def _even_proj(hn, w_in):
    B, T, K = hn.shape
    M = B * T
    cols = jnp.split(w_in.astype(_BF16), EVEN_CUTS, axis=-1)
    wide = [0, 1, 2, 4, 5, 6, 7, 8, 9]
    narrow = [3, 10, 11]
    seg = W_FOX
    w_main = jnp.concatenate([cols[i] for i in wide], axis=-1)
    w_small = jnp.concatenate([cols[i] for i in narrow], axis=-1)
    n_small = w_small.shape[1]
    w_small = jnp.pad(w_small, ((0, 0), (0, V7X_LANES - n_small)))
    tm = min(512, M)
    outs = pl.pallas_call(
        functools.partial(_even_proj_kernel, seg=seg),
        grid=(M // tm,),
        in_specs=[pl.BlockSpec((tm, K), lambda i: (i, 0)),
                  pl.BlockSpec((K, seg * len(wide)), lambda i: (0, 0)),
                  pl.BlockSpec((K, V7X_LANES), lambda i: (0, 0))],
        out_specs=[pl.BlockSpec((tm, seg), lambda i: (i, 0))] * len(wide)
        + [pl.BlockSpec((tm, V7X_LANES), lambda i: (i, 0))],
        out_shape=[jax.ShapeDtypeStruct((M, seg), _F32)] * len(wide)
        + [jax.ShapeDtypeStruct((M, V7X_LANES), _F32)],
        compiler_params=pltpu.CompilerParams(
            dimension_semantics=("parallel",),
            vmem_limit_bytes=V7X_VMEM_LIMIT_BYTES),
        name="even_in_proj",
    )(hn.reshape(M, K), w_main, w_small)
    groups = [None] * len(EVEN_SIZES)
    for i, o in zip(wide, outs[:-1]):
        groups[i] = o.reshape(B, T, seg)
    off = 0
    for i in narrow:
        groups[i] = outs[-1][:, off:off + EVEN_SIZES[i]].reshape(B, T, EVEN_SIZES[i])
        off += EVEN_SIZES[i]
    return groups


def _mm(x, w):
    M, K = x.shape
    N = w.shape[1]
    n_pad = (-N) % V7X_LANES
    wb = w.astype(_BF16)
    if n_pad:
        wb = jnp.pad(wb, ((0, 0), (0, n_pad)))
    Np = N + n_pad
    tm = min(512, M)
    tn = Np
    for cand in (1024, 768, 512, 384, 256, 128):
        if Np % cand == 0:
            tn = cand
            break
    assert M % tm == 0
    out = pl.pallas_call(
        _mm_kernel,
        grid=(M // tm, Np // tn),
        in_specs=[pl.BlockSpec((tm, K), lambda i, j: (i, 0)),
                  pl.BlockSpec((K, tn), lambda i, j: (0, j))],
        out_specs=pl.BlockSpec((tm, tn), lambda i, j: (i, j)),
        out_shape=jax.ShapeDtypeStruct((M, Np), _F32),
        compiler_params=pltpu.CompilerParams(
            dimension_semantics=("parallel", "parallel"),
            vmem_limit_bytes=V7X_VMEM_LIMIT_BYTES),
        name="proj_matmul",
    )(x, wb)
    return out[:, :N] if n_pad else out


def _mm3(x, w):
    B, T, K = x.shape
    return _mm(x.reshape(B * T, K), w).reshape(B, T, w.shape[1])


def _causal_pairs(S, tq, tk):
    qi, ki = [], []
    for q in range(S // tq):
        for k in range((q * tq + tq - 1) // tk + 1):
            qi.append(q)
            ki.append(k)
    return jnp.asarray(qi, jnp.int32), jnp.asarray(ki, jnp.int32)


LOG2E = 1.4426950408889634


def _online_softmax_step(s, v, m_sc, l_sc, acc_sc, h):
    m_old = m_sc[h]
    m_new = jnp.maximum(m_old, jnp.max(s, axis=-1, keepdims=True))
    alpha = jnp.exp2(m_old - m_new)
    p = jnp.exp2(s - m_new)
    l_sc[h] = alpha * l_sc[h] + jnp.sum(p, axis=-1, keepdims=True)
    acc_sc[h] = alpha * acc_sc[h] + _dot(p.astype(_BF16), v)
    m_sc[h] = m_new


def _attend_heads(q_ref, k_ref, v_ref, m_sc, l_sc, acc_sc, nh, keep, bias=None):
    s_next = _dot_nt(q_ref[0, 0], k_ref[0, 0])
    for h in range(nh):
        s = s_next
        if h + 1 < nh:
            s_next = _dot_nt(q_ref[0, h + 1], k_ref[0, h + 1])
        if bias is not None:
            s = s + bias[0][0, h] - bias[1][0, h]
        if keep is not None:
            s = jnp.where(keep, s, NEG_BIG)
        _online_softmax_step(s, v_ref[0, h], m_sc, l_sc, acc_sc, h)


def _attn_init(ki, m_sc, l_sc, acc_sc):
    @pl.when(ki == 0)
    def _():
        m_sc[...] = jnp.full(m_sc.shape, NEG_BIG, _F32)
        l_sc[...] = jnp.zeros(l_sc.shape, _F32)
        acc_sc[...] = jnp.zeros(acc_sc.shape, _F32)


def _attn_finish(ki, k_last, o_ref, l_sc, acc_sc, nh):
    @pl.when(ki == k_last)
    def _():
        for h in range(nh):
            o_ref[0, h] = acc_sc[h] / l_sc[h]


def _fox_attn_kernel(qi_tab, ki_tab, q_ref, k_ref, v_ref, cq_ref, ck_ref, o_ref, m_sc, l_sc, acc_sc,
                     *, tq, tk, nh):
    p = pl.program_id(1)
    qi = qi_tab[p]
    ki = ki_tab[p]
    k_last = (qi * tq + tq - 1) // tk
    _attn_init(ki, m_sc, l_sc, acc_sc)
    crosses_diagonal = ki * tk + tk - 1 > qi * tq
    bias = (cq_ref, ck_ref)

    @pl.when(crosses_diagonal)
    def _():
        rows = qi * tq + lax.broadcasted_iota(jnp.int32, (tq, tk), 0)
        cols = ki * tk + lax.broadcasted_iota(jnp.int32, (tq, tk), 1)
        _attend_heads(q_ref, k_ref, v_ref, m_sc, l_sc, acc_sc, nh, cols <= rows, bias)

    @pl.when(jnp.logical_not(crosses_diagonal))
    def _():
        _attend_heads(q_ref, k_ref, v_ref, m_sc, l_sc, acc_sc, nh, None, bias)

    _attn_finish(ki, k_last, o_ref, l_sc, acc_sc, nh)


def _fox_prompt(q, k, v, logf, tq=512, tk=1024):
    B, S, H, dh = q.shape
    tq, tk = min(tq, S), min(tk, S)
    ct = jnp.swapaxes(jnp.cumsum(logf, axis=1) * LOG2E, 1, 2)
    cq = ct[..., None]
    ck = ct[:, :, None, :]
    qh = jnp.swapaxes(q * (HEAD_DIM ** -0.5 * LOG2E), 1, 2).astype(_BF16)
    kh = jnp.swapaxes(k, 1, 2).astype(_BF16)
    vh = jnp.swapaxes(v, 1, 2).astype(_BF16)
    qi_tab, ki_tab = _causal_pairs(S, tq, tk)
    grid_spec = pltpu.PrefetchScalarGridSpec(
        num_scalar_prefetch=2,
        grid=(B, int(qi_tab.shape[0])),
        in_specs=[
            pl.BlockSpec((1, H, tq, dh), lambda b, p, qt, kt: (b, 0, qt[p], 0)),
            pl.BlockSpec((1, H, tk, dh), lambda b, p, qt, kt: (b, 0, kt[p], 0)),
            pl.BlockSpec((1, H, tk, dh), lambda b, p, qt, kt: (b, 0, kt[p], 0)),
            pl.BlockSpec((1, H, tq, 1), lambda b, p, qt, kt: (b, 0, qt[p], 0)),
            pl.BlockSpec((1, H, 1, tk), lambda b, p, qt, kt: (b, 0, 0, kt[p])),
        ],
        out_specs=pl.BlockSpec((1, H, tq, dh), lambda b, p, qt, kt: (b, 0, qt[p], 0)),
        scratch_shapes=[pltpu.VMEM((H, tq, 1), _F32), pltpu.VMEM((H, tq, 1), _F32),
                        pltpu.VMEM((H, tq, dh), _F32)],
    )
    out = pl.pallas_call(
        functools.partial(_fox_attn_kernel, tq=tq, tk=tk, nh=H),
        grid_spec=grid_spec,
        out_shape=jax.ShapeDtypeStruct((B, H, S, dh), _F32),
        compiler_params=pltpu.CompilerParams(
            dimension_semantics=("parallel", "arbitrary"),
            vmem_limit_bytes=V7X_VMEM_LIMIT_BYTES),
        name="fox_prompt_attn",
    )(qi_tab, ki_tab, qh, kh, vh, cq, ck)
    return jnp.swapaxes(out, 1, 2).reshape(B, S, H * dh)


def _sortable_key(x):
    b = pltpu.bitcast(x, jnp.int32)
    return b ^ ((b >> 31) & jnp.int32(0x7FFFFFFF))


def _topk_rank_threshold(count, nrows, topk):
    kf = float(topk)

    def bit_step(i, v):
        cand = v + jnp.left_shift(jnp.int32(1), 31 - i)
        cnt = count(lambda kk, cols: kk >= cand)
        return jnp.where(cnt >= kf, cand, v)

    v = lax.fori_loop(0, 32, bit_step, jnp.full((nrows, 1), INT_MIN, jnp.int32))
    n_gt = count(lambda kk, cols: kk > v)
    n_ge = count(lambda kk, cols: kk >= v)
    need = kf - n_gt
    has_thr = v != jnp.int32(INT_MIN)
    ties = jnp.max(jnp.where(has_thr & (n_ge > kf), 1.0, 0.0)) > 0.0

    def tie_search(_):
        def idx_step(i, m):
            cand = m + jnp.left_shift(jnp.int32(1), 30 - i)
            cnt = count(lambda kk, cols: (kk == v) & (cols < cand))
            return jnp.where(cnt < need, cand, m)
        return lax.fori_loop(0, 31, idx_step, jnp.zeros((nrows, 1), jnp.int32))

    m_last = lax.cond(ties, tie_search, lambda _: jnp.full((nrows, 1), 2 ** 31 - 1, jnp.int32), 0)
    return v, m_last


def _topk_keep(kk, cols, v, m_last):
    return ((kk > v) | ((kk == v) & (cols <= m_last))) & (kk != jnp.int32(INT_MIN))


def _dsa_select_kernel(iq_ref, iw_ref, ikt_ref, mask_ref, key_sc, *, tq, tk, nk, nh, topk):
    qi = pl.program_id(1)
    nkc = (qi * tq + tq - 1) // tk + 1
    rows = qi * tq + lax.broadcasted_iota(jnp.int32, (tq, tk), 0)
    col0 = lax.broadcasted_iota(jnp.int32, (tq, tk), 1)
    iw = iw_ref[0]

    def score_chunk(c, carry):
        kt = ikt_ref[0, c]
        acc = jnp.zeros((tq, tk), _F32)
        for h in range(nh):
            d = jnp.dot(iq_ref[0, h], kt, preferred_element_type=_F32)
            acc = acc + jnp.maximum(d, 0.0) * iw[:, h:h + 1]
        key = _sortable_key(acc)
        key_sc[c] = jnp.where(c * tk + col0 <= rows, key, jnp.int32(INT_MIN))
        return carry

    lax.fori_loop(0, nkc, score_chunk, 0)

    def count(pred):
        def body(c, part):
            hit = jnp.where(pred(key_sc[c], c * tk + col0), 1.0, 0.0)
            for j in range(tk // V7X_LANES):
                part = part + hit[:, j * V7X_LANES:(j + 1) * V7X_LANES]
            return part
        part = lax.fori_loop(0, nkc, body, jnp.zeros((tq, V7X_LANES), _F32))
        return jnp.sum(part, axis=-1, keepdims=True)

    v, m_last = _topk_rank_threshold(count, tq, topk)

    def write_chunk(c, carry):
        sel = _topk_keep(key_sc[c], c * tk + col0, v, m_last)
        mask_ref[0, 0, c] = sel.astype(jnp.int8)
        return carry

    lax.fori_loop(0, nkc, write_chunk, 0)

    def zero_chunk(c, carry):
        mask_ref[0, 0, c] = jnp.zeros((tq, tk), jnp.int8)
        return carry

    lax.fori_loop(nkc, nk, zero_chunk, 0)


def _dsa_select(iq, ik, iw, topk, tq=256, tk=1024):
    B, S, Hi, Di = iq.shape
    tq, tk = min(tq, S), min(tk, S)
    nq, nk = S // tq, S // tk
    iqh = jnp.swapaxes(iq, 1, 2).astype(_BF16)
    ikt = jnp.swapaxes(ik.reshape(B, nk, tk, Di), 2, 3).astype(_BF16)
    iws = (iw * (D_IDX ** -0.5)).astype(_F32)
    return pl.pallas_call(
        functools.partial(_dsa_select_kernel, tq=tq, tk=tk, nk=nk, nh=Hi, topk=topk),
        grid=(B, nq),
        in_specs=[pl.BlockSpec((1, Hi, tq, Di), lambda b, q: (b, 0, q, 0)),
                  pl.BlockSpec((1, tq, Hi), lambda b, q: (b, q, 0)),
                  pl.BlockSpec((1, nk, Di, tk), lambda b, q: (b, 0, 0, 0))],
        out_specs=pl.BlockSpec((1, 1, nk, tq, tk), lambda b, q: (b, q, 0, 0, 0)),
        out_shape=jax.ShapeDtypeStruct((B, nq, nk, tq, tk), jnp.int8),
        scratch_shapes=[pltpu.VMEM((nk, tq, tk), jnp.int32)],
        compiler_params=pltpu.CompilerParams(
            dimension_semantics=("parallel", "arbitrary"),
            vmem_limit_bytes=V7X_VMEM_LIMIT_BYTES),
        name="dsa_topk_select",
    )(iqh, iws, ikt)


def _dsa_attn_kernel(qi_tab, ki_tab, q_ref, k_ref, v_ref, mask_ref, o_ref, m_sc, l_sc, acc_sc,
                     *, tq, tk, nh):
    p = pl.program_id(1)
    qi = qi_tab[p]
    ki = ki_tab[p]
    k_last = (qi * tq + tq - 1) // tk
    _attn_init(ki, m_sc, l_sc, acc_sc)
    keep = mask_ref[0, :, 0].reshape(tq, tk) != 0
    _attend_heads(q_ref, k_ref, v_ref, m_sc, l_sc, acc_sc, nh, keep)
    _attn_finish(ki, k_last, o_ref, l_sc, acc_sc, nh)


def _dsa_prompt(q, k, v, iq, ik, iw, tq=512, tk=1024, tq_sel=128):
    B, S, H, dh = q.shape
    topk = min(TOPK_MAX, S // 4)
    tq, tk, tq_sel = min(tq, S), min(tk, S), min(tq_sel, S)
    mask = _dsa_select(iq, ik, iw, topk, tq=tq_sel, tk=tk)
    r = tq // tq_sel
    qh = jnp.swapaxes(q * (HEAD_DIM ** -0.5 * LOG2E), 1, 2).astype(_BF16)
    kh = jnp.swapaxes(k, 1, 2).astype(_BF16)
    vh = jnp.swapaxes(v, 1, 2).astype(_BF16)
    qi_tab, ki_tab = _causal_pairs(S, tq, tk)
    grid_spec = pltpu.PrefetchScalarGridSpec(
        num_scalar_prefetch=2,
        grid=(B, int(qi_tab.shape[0])),
        in_specs=[
            pl.BlockSpec((1, H, tq, dh), lambda b, p, qt, kt: (b, 0, qt[p], 0)),
            pl.BlockSpec((1, H, tk, dh), lambda b, p, qt, kt: (b, 0, kt[p], 0)),
            pl.BlockSpec((1, H, tk, dh), lambda b, p, qt, kt: (b, 0, kt[p], 0)),
            pl.BlockSpec((1, r, 1, tq_sel, tk), lambda b, p, qt, kt: (b, qt[p], kt[p], 0, 0)),
        ],
        out_specs=pl.BlockSpec((1, H, tq, dh), lambda b, p, qt, kt: (b, 0, qt[p], 0)),
        scratch_shapes=[pltpu.VMEM((H, tq, 1), _F32), pltpu.VMEM((H, tq, 1), _F32),
                        pltpu.VMEM((H, tq, dh), _F32)],
    )
    out = pl.pallas_call(
        functools.partial(_dsa_attn_kernel, tq=tq, tk=tk, nh=H),
        grid_spec=grid_spec,
        out_shape=jax.ShapeDtypeStruct((B, H, S, dh), _F32),
        compiler_params=pltpu.CompilerParams(
            dimension_semantics=("parallel", "arbitrary"),
            vmem_limit_bytes=V7X_VMEM_LIMIT_BYTES),
        name="dsa_prompt_attn",
    )(qi_tab, ki_tab, qh, kh, vh, mask)
    return jnp.swapaxes(out, 1, 2).reshape(B, S, H * dh)


def _split2(x):
    hi = x.astype(_BF16)
    lo = (x - hi.astype(_F32)).astype(_BF16)
    return hi, lo


def _dot3(a, b):
    ah, al = a
    bh, bl = b
    return _dot(ah, bh) + _dot(ah, bl) + _dot(al, bh)


def _wkv_chunk_kernel(r_ref, lw_ref, k_ref, v_ref, al_ref, be_ref, s0_ref, y_ref, sT_ref, s_sc,
                      *, L, n, npair, nchunk):
    t_blk = pl.program_id(2)
    L2, n2 = 2 * L, 2 * n

    @pl.when(t_blk == 0)
    def _():
        s_sc[...] = s0_ref[0]

    ri = lax.broadcasted_iota(jnp.int32, (L2, L2), 0)
    ci = lax.broadcasted_iota(jnp.int32, (L2, L2), 1)
    same_head = (ri // L) == (ci // L)
    strict = same_head & (ci < ri)
    incl = same_head & (ci <= ri)
    bs = min(16, L)
    diag_blk = (ri // bs) == (ci // bs)
    eye = (ri == ci).astype(_F32)
    cum_tri = (lax.broadcasted_iota(jnp.int32, (L, L), 0)
               >= lax.broadcasted_iota(jnp.int32, (L, L), 1)).astype(_BF16)
    sr = lax.broadcasted_iota(jnp.int32, (n2, n2), 0)
    sc = lax.broadcasted_iota(jnp.int32, (n2, n2), 1)
    state_blk = (sr // n) == (sc // n)
    lane_a = lax.broadcasted_iota(jnp.int32, (L, n2), 1) < n

    def expand(x):
        return jnp.concatenate([jnp.where(lane_a, x, 0.0), jnp.where(lane_a, 0.0, x)], axis=0)

    def compact(x):
        return x[:L] + x[L:]

    def each(fn, *cols):
        return [fn(*args) for args in zip(*cols)]

    def chunk(c, carry):
        rows = pl.ds(pl.multiple_of(c * L, L), L)
        lane_sl = [slice(p * n2, (p + 1) * n2) for p in range(npair)]
        lw = [lw_ref[0, rows, ls] for ls in lane_sl]
        r = [r_ref[0, rows, ls] for ls in lane_sl]
        k = [k_ref[0, rows, ls] for ls in lane_sl]
        v = [v_ref[0, rows, ls] for ls in lane_sl]
        al = [al_ref[0, rows, ls] for ls in lane_sl]
        be = [be_ref[0, rows, ls] for ls in lane_sl]

        def cumsum(x):
            hi = x.astype(_BF16)
            r1 = x - hi.astype(_F32)
            mid = r1.astype(_BF16)
            lo = (r1 - mid.astype(_F32)).astype(_BF16)
            return _dot(cum_tri, hi) + _dot(cum_tri, mid) + _dot(cum_tri, lo)

        g = each(cumsum, lw)
        g_last = each(lambda x: x[L - 1:L, :], g)
        a_t = each(lambda x, gg, l: x * jnp.exp(gg - l), al, g, lw)
        r_t = each(lambda x, gg: x * jnp.exp(gg), r, g)
        eng = each(lambda gg: jnp.exp(-gg), g)
        pa_pr = each(lambda x, y: jnp.concatenate([expand(x), expand(y)], axis=0).astype(_BF16), a_t, r_t)
        pb = each(lambda x, e: expand(x * e).astype(_BF16), be, eng)
        pk = each(lambda x, e: expand(x * e).astype(_BF16), k, eng)
        pv = each(lambda x: expand(x).astype(_BF16), v)
        xb = each(_dot_nt, pa_pr, pb)
        xk = each(_dot_nt, pa_pr, pk)
        m_ab = each(lambda x: jnp.where(strict, x[:L2], 0.0), xb)
        m_ak = each(lambda x: jnp.where(strict, x[:L2], 0.0).astype(_BF16), xk)
        n_rb = each(lambda x: jnp.where(incl, x[L2:], 0.0).astype(_BF16), xb)
        n_rk = each(lambda x: jnp.where(incl, x[L2:], 0.0).astype(_BF16), xk)
        d1 = each(lambda x: jnp.where(diag_blk, x, 0.0), m_ab)
        e_s = each(lambda x, d: _split2(x - d), m_ab, d1)
        d1s = each(_split2, d1)
        d2 = each(_dot3, d1s, d1s)
        d2s = each(_split2, d2)
        d4 = each(_dot3, d2s, d2s)
        d4s = each(_split2, d4)
        d8 = each(_dot3, d4s, d4s)
        p12 = each(lambda x, y: _dot3(_split2(eye + x), _split2(eye + y)), d1, d2)
        p48 = each(lambda x, y: _dot3(_split2(eye + x), _split2(eye + y)), d4, d8)
        xs = each(lambda x, y: _split2(_dot3(_split2(x), _split2(y))), p12, p48)
        nn = each(_dot3, xs, e_s)
        nns = each(_split2, nn)
        nn2 = each(_dot3, nns, nns)
        q12 = each(lambda x, y: _dot3(_split2(eye + x), _split2(eye + y)), nn, nn2)
        tinv = each(lambda x, y: _dot3(_split2(x), y).astype(_BF16), q12, xs)
        s = [s_sc[p] for p in range(npair)]
        ars = each(lambda x, y, st: _dot_nt(jnp.concatenate([x, y], axis=0).astype(_BF16), st.astype(_BF16)),
                   a_t, r_t, s)
        rhs = each(lambda x, m, vv: (expand(x[:L]) + _dot(m, vv)).astype(_BF16), ars, m_ak, pv)
        u_e = each(_dot, tinv, rhs)
        y_e = each(lambda x, nb, u, nk, vv: expand(x[L:]) + _dot(nb, u.astype(_BF16)) + _dot(nk, vv),
                   ars, n_rb, u_e, n_rk, pv)
        for ls, ye in zip(lane_sl, y_e):
            y_ref[0, rows, ls] = compact(ye)
        dec = each(lambda gl, gg: jnp.exp(gl - gg), g_last, g)
        uv = each(lambda u, vv: jnp.concatenate([compact(u), vv], axis=0).astype(_BF16), u_e, v)
        bk = each(lambda x, y, d: jnp.concatenate([x * d, y * d], axis=0).astype(_BF16), be, k, dec)
        upd = each(_dot_tn, uv, bk)
        for p in range(npair):
            s_sc[p] = s[p] * jnp.exp(g_last[p]) + jnp.where(state_blk, upd[p], 0.0)
        return carry

    lax.fori_loop(0, nchunk, chunk, 0)

    @pl.when(t_blk == pl.num_programs(2) - 1)
    def _():
        sT_ref[0] = s_sc[...]


def _wkv7(s0, r, lw, k, v, kk, a, L=64, tc=256, npair=8):
    B, T, H, N = r.shape
    L, tc = min(L, T), min(tc, T)
    npair = min(npair, H // 2)
    D = H * N
    flat = lambda x: x.reshape(B, T, D)
    s0p = s0.reshape(B, H // 2, 2, N, N)
    z = jnp.zeros_like(s0p[:, :, 0])
    s0bd = jnp.concatenate([jnp.concatenate([s0p[:, :, 0], z], axis=-1),
                            jnp.concatenate([z, s0p[:, :, 1]], axis=-1)], axis=-2)
    seq_spec = pl.BlockSpec((1, tc, npair * 2 * N), lambda b, g, t: (b, t, g))
    st_spec = pl.BlockSpec((1, npair, 2 * N, 2 * N), lambda b, g, t: (b, g, 0, 0))
    y, sT = pl.pallas_call(
        functools.partial(_wkv_chunk_kernel, L=L, n=N, npair=npair, nchunk=tc // L),
        grid=(B, H // (2 * npair), T // tc),
        in_specs=[seq_spec] * 6 + [st_spec],
        out_specs=[seq_spec, st_spec],
        out_shape=[jax.ShapeDtypeStruct((B, T, D), _F32),
                   jax.ShapeDtypeStruct((B, H // 2, 2 * N, 2 * N), _F32)],
        scratch_shapes=[pltpu.VMEM((npair, 2 * N, 2 * N), _F32)],
        compiler_params=pltpu.CompilerParams(
            dimension_semantics=("parallel", "parallel", "arbitrary"),
            vmem_limit_bytes=V7X_VMEM_LIMIT_BYTES),
        name="wkv7_chunked",
    )(flat(r), flat(lw), flat(k), flat(v), flat(-kk), flat(kk * a), s0bd)
    sT = jnp.stack([sT[:, :, :N, :N], sT[:, :, N:, N:]], axis=2).reshape(B, H, N, N)
    return sT, y.reshape(B, T, H, N)


def _rmsnorm(x, g):
    y = x * lax.rsqrt(jnp.mean(x * x, axis=-1, keepdims=True) + RMS_EPS)
    return y * g


def _partial_rope(x, pos):
    half = ROPE_DIM // 2
    inv = ROPE_THETA ** (-2.0 * jnp.arange(half, dtype=_F32) / ROPE_DIM)
    ang = pos.astype(_F32)[:, None] * inv
    cos, sin = jnp.cos(ang)[:, None, :], jnp.sin(ang)[:, None, :]
    x1, x2 = x[..., :half], x[..., half:ROPE_DIM]
    return jnp.concatenate([x1 * cos - x2 * sin, x2 * cos + x1 * sin, x[..., ROPE_DIM:]], axis=-1)


def _even_project(hn, w_in, b_forget, pos):
    B, T, _ = hn.shape
    fq, fk, fv, ff, fg, dq, dk, dv, dg, iq, ik, iw = _even_proj(hn, w_in)

    def heads(t, h):
        return t.reshape(B, T, h, -1)

    logf = jax.nn.log_sigmoid(ff + b_forget)
    dq = _partial_rope(heads(dq, H_DSA), pos)
    dk = _partial_rope(heads(dk, H_DSA), pos)
    iq = _partial_rope(heads(iq, H_IDX), pos)
    ik = _partial_rope(ik[:, :, None, :], pos)[:, :, 0]
    iw = iw * (H_IDX ** -0.5)
    return (heads(fq, H_FOX), heads(fk, H_FOX), heads(fv, H_FOX), logf, jax.nn.silu(fg),
            dq, dk, heads(dv, H_DSA), jax.nn.silu(dg), iq, ik, iw)


PAGES_PER_STEP = 8


def _paged_attn_kernel(pt_ref, q_ref, rh_ref, roff_ref, bias_ref, *refs, pg, span, nh):
    k_refs, v_refs = refs[:pg], refs[pg:2 * pg]
    kn_ref, vn_ref, bn_ref, o_ref, m_sc, l_sc, acc_sc = refs[2 * pg:]
    j = pl.program_id(1)

    @pl.when(j == 0)
    def _():
        m_sc[...] = jnp.full(m_sc.shape, NEG_BIG, _F32)
        l_sc[...] = jnp.zeros(l_sc.shape, _F32)
        acc_sc[...] = jnp.zeros(acc_sc.shape, _F32)

    q = q_ref[0]
    nrows = q.shape[0]
    roff = roff_ref[0]
    nrep = nrows // bias_ref.shape[1]
    own_head = (lax.broadcasted_iota(jnp.int32, (nrows, span), 1) % nh) == rh_ref[...]

    def update(scores, values):
        m_old = m_sc[...]
        m_new = m_old
        for s in scores:
            m_new = jnp.maximum(m_new, jnp.max(s, axis=-1, keepdims=True))
        alpha = jnp.exp(m_old - m_new)
        l_new = alpha * l_sc[...]
        acc = alpha * acc_sc[...]
        for s, v in zip(scores, values):
            p = jnp.exp(s - m_new)
            l_new = l_new + jnp.sum(p, axis=-1, keepdims=True)
            acc = acc + _dot(p.astype(_BF16), v)
        l_sc[...] = l_new
        acc_sc[...] = acc
        m_sc[...] = m_new

    scores = []
    for i in range(pg):
        b = bias_ref[0, :, i * span:(i + 1) * span]
        b = jnp.concatenate([b] * nrep, axis=0) if nrep > 1 else b
        s = _dot_nt(q, k_refs[i][0, 0].astype(_BF16)) + roff + b
        scores.append(jnp.where(own_head, s, NEG_BIG))
    update(scores, [v_refs[i][0, 0].astype(_BF16) for i in range(pg)])

    @pl.when(j == pl.num_programs(1) - 1)
    def _():
        update([_dot_nt(q, kn_ref[0].astype(_BF16)) + bn_ref[0]], [vn_ref[0].astype(_BF16)])
        o_ref[0] = acc_sc[...] / l_sc[...]


def _paged_attention(q, row_head, roff, bias, cache_k, cache_v, page_table, layer, k_new, v_new, bias_new):
    B, R, dh = q.shape
    n_pool, n_even, page, nh, _ = cache_k.shape
    span = page * nh
    cache_k = cache_k.reshape(n_pool, n_even, span, dh)
    cache_v = cache_v.reshape(n_pool, n_even, span, dh)
    npages = page_table.shape[1]
    pg = min(PAGES_PER_STEP, npages)
    assert npages % pg == 0
    tn = k_new.shape[1]
    tile_rows = bias.shape[1]

    def page_spec(i):
        return pl.BlockSpec((1, 1, span, dh),
                            lambda b, j, pt, i=i: (pt[b * npages + j * pg + i], layer, 0, 0))

    grid_spec = pltpu.PrefetchScalarGridSpec(
        num_scalar_prefetch=1,
        grid=(B, npages // pg),
        in_specs=[pl.BlockSpec((1, R, dh), lambda b, j, pt: (b, 0, 0)),
                  pl.BlockSpec((R, 1), lambda b, j, pt: (0, 0)),
                  pl.BlockSpec((1, R, 1), lambda b, j, pt: (b, 0, 0)),
                  pl.BlockSpec((1, tile_rows, pg * span), lambda b, j, pt: (b, 0, j))]
        + [page_spec(i) for i in range(pg)] + [page_spec(i) for i in range(pg)]
        + [pl.BlockSpec((1, tn, dh), lambda b, j, pt: (b, 0, 0)),
           pl.BlockSpec((1, tn, dh), lambda b, j, pt: (b, 0, 0)),
           pl.BlockSpec((1, R, tn), lambda b, j, pt: (b, 0, 0))],
        out_specs=pl.BlockSpec((1, R, dh), lambda b, j, pt: (b, 0, 0)),
        scratch_shapes=[pltpu.VMEM((R, 1), _F32), pltpu.VMEM((R, 1), _F32), pltpu.VMEM((R, dh), _F32)],
    )
    return pl.pallas_call(
        functools.partial(_paged_attn_kernel, pg=pg, span=span, nh=nh),
        grid_spec=grid_spec,
        out_shape=jax.ShapeDtypeStruct((B, R, dh), _F32),
        compiler_params=pltpu.CompilerParams(
            dimension_semantics=("parallel", "arbitrary"),
            vmem_limit_bytes=V7X_VMEM_LIMIT_BYTES),
        name="paged_decode_attn",
    )(page_table.reshape(-1), q, row_head, roff, bias, *([cache_k] * pg), *([cache_v] * pg),
      k_new, v_new, bias_new)


def _pad_rows(x, rows, value=0.0):
    return jnp.pad(x, ((0, 0), (0, rows - x.shape[1]), (0, 0)), constant_values=value)


def _fox_sample(q, k, v, logf, cache_k, cache_v, cache_logf, page_table, layer):
    B, T, H, dh = q.shape
    R = T * H
    logf_past = cache_logf[page_table, layer].reshape(B, -1, H)
    c_past = jnp.cumsum(logf_past, axis=1)
    c_new = c_past[:, -1:] + jnp.cumsum(logf, axis=1)
    row_head = (jnp.arange(R, dtype=jnp.int32) % H).reshape(R, 1)
    keep = ((jnp.arange(T)[None, :] <= jnp.arange(T)[:, None])[:, None, :, None]
            & jnp.eye(H, dtype=bool)[None, :, None, :])
    bn = c_new[:, :, :, None, None] - c_new[:, None, None, :, :]
    bn = jnp.where(keep[None], bn, NEG_BIG).reshape(B, R, R)
    out = _paged_attention((q * (HEAD_DIM ** -0.5)).reshape(B, R, dh).astype(_BF16), row_head,
                           c_new.reshape(B, R, 1), -c_past.reshape(B, 1, -1),
                           cache_k, cache_v, page_table, layer,
                           k.reshape(B, R, dh), v.reshape(B, R, dh), bn)
    return out.reshape(B, T, H * dh)


def _dsa_sample_select_kernel(pt_ref, x_ref, w_ref, kn_ref, *refs, npages, page, nq, nh, topk):
    kp_refs = refs[:npages]
    bias_ref, key_sc = refs[npages:]
    i = pl.program_id(1)
    x = x_ref[0]
    w = w_ref[0]
    r0 = pl.multiple_of(i * nq, nq)

    def fold(d):
        t = jnp.maximum(d, 0.0) * w
        acc = t[0:nq]
        for h in range(1, nh):
            acc = acc + t[h * nq:(h + 1) * nq]
        return acc

    for p in range(npages):
        d = _dot_nt(x, kp_refs[p][0, 0].astype(_BF16))
        key_sc[pl.ds(r0, nq), p * page:(p + 1) * page] = _sortable_key(fold(d))
    dn = _dot_nt(x, kn_ref[0].astype(_BF16))
    qrow = lax.broadcasted_iota(jnp.int32, (nq, page), 0)
    col = lax.broadcasted_iota(jnp.int32, (nq, page), 1)
    key_sc[pl.ds(r0, nq), npages * page:(npages + 1) * page] = jnp.where(
        col <= qrow, _sortable_key(fold(dn)), jnp.int32(INT_MIN))

    @pl.when(i == pl.num_programs(1) - 1)
    def _():
        nrows, ncols = key_sc.shape
        cols = lax.broadcasted_iota(jnp.int32, (nrows, ncols), 1)

        def count(pred):
            return jnp.sum(jnp.where(pred(key_sc[...], cols), 1.0, 0.0), axis=-1, keepdims=True)

        v, m_last = _topk_rank_threshold(count, nrows, topk)
        bias_ref[0] = jnp.where(_topk_keep(key_sc[...], cols, v, m_last), 0.0, NEG_BIG)


def _dsa_sample_select(iq, ik, iw, cache_kidx, page_table, layer, topk, gb=8):
    B, T, Hi, Di = iq.shape
    npages = page_table.shape[1]
    gb = min(gb, B)
    assert B % gb == 0
    ncols = (npages + 1) * PAGE_SIZE
    x = jnp.swapaxes(iq, 1, 2).reshape(B, Hi * T, Di).astype(_BF16)
    w = jnp.swapaxes(iw * (D_IDX ** -0.5), 1, 2).reshape(B, Hi * T, 1)
    kn = _pad_rows(ik, PAGE_SIZE)

    def page_spec(p):
        return pl.BlockSpec((1, 1, PAGE_SIZE, Di),
                            lambda g, i, pt, p=p: (pt[(g * gb + i) * npages + p], layer, 0, 0))

    grid_spec = pltpu.PrefetchScalarGridSpec(
        num_scalar_prefetch=1,
        grid=(B // gb, gb),
        in_specs=[pl.BlockSpec((1, Hi * T, Di), lambda g, i, pt: (g * gb + i, 0, 0)),
                  pl.BlockSpec((1, Hi * T, 1), lambda g, i, pt: (g * gb + i, 0, 0)),
                  pl.BlockSpec((1, PAGE_SIZE, Di), lambda g, i, pt: (g * gb + i, 0, 0))]
        + [page_spec(p) for p in range(npages)],
        out_specs=pl.BlockSpec((1, gb * T, ncols), lambda g, i, pt: (g, 0, 0)),
        scratch_shapes=[pltpu.VMEM((gb * T, ncols), jnp.int32)],
    )
    bias = pl.pallas_call(
        functools.partial(_dsa_sample_select_kernel, npages=npages, page=PAGE_SIZE, nq=T, nh=Hi, topk=topk),
        grid_spec=grid_spec,
        out_shape=jax.ShapeDtypeStruct((B // gb, gb * T, ncols), _F32),
        compiler_params=pltpu.CompilerParams(
            dimension_semantics=("parallel", "arbitrary"),
            vmem_limit_bytes=V7X_VMEM_LIMIT_BYTES),
        name="dsa_sample_select",
    )(page_table.reshape(-1), x, w, kn, *([cache_kidx] * npages))
    return bias.reshape(B, T, ncols)


def _dsa_sample(q, k, v, iq, ik, iw, cache_k, cache_v, cache_kidx, page_table, layer):
    B, T, H, dh = q.shape
    R = H * T
    P = page_table.shape[1] * PAGE_SIZE
    topk = min(TOPK_MAX, (P + T) // 4)
    sel = _dsa_sample_select(iq, ik, iw, cache_kidx, page_table, layer, topk)
    row_head = (jnp.arange(R, dtype=jnp.int32) // T).reshape(R, 1)
    qh = jnp.swapaxes(q * (HEAD_DIM ** -0.5), 1, 2).reshape(B, R, dh).astype(_BF16)
    bias = jnp.repeat(sel[:, :, :P], H, axis=-1)
    own = jnp.eye(H, dtype=bool)[:, None, None, :]
    bn = jnp.where(own[None], sel[:, None, :, P:P + T, None], NEG_BIG).reshape(B, R, T * H)
    out = _paged_attention(qh, row_head, jnp.zeros((B, R, 1), _F32), bias,
                           cache_k, cache_v, page_table, layer,
                           k.reshape(B, T * H, dh), v.reshape(B, T * H, dh), bn)
    return jnp.swapaxes(out.reshape(B, H, T, dh), 1, 2).reshape(B, T, H * dh)


def _rwkv_mix(xn, x_prev, s0, v_first, vres, mu, w_rkvg, w_o, w_d0, w_d1, w_d2, w_a0, w_a1, w_a2,
              k_k, k_a, r_k, ln_w, ln_b):
    B, T, D = xn.shape
    H = D // HEAD_DIM
    dx = jnp.concatenate([x_prev[:, None, :], xn[:, :-1]], axis=1) - xn
    xr, xw, xk, xv, xa, xg = [xn + dx * mu[j] for j in range(6)]
    r = _mm3(xr, w_rkvg[0])
    k = _mm3(xk, w_rkvg[1])
    v = _mm3(xv, w_rkvg[2])
    gate = jax.nn.silu(_mm3(xg, w_rkvg[3]))
    w_log = -jax.nn.softplus(-(w_d0 + _mm3(jnp.tanh(_mm3(xw, w_d1)), w_d2))) - 0.5
    log_decay = -jnp.exp(w_log)
    a = jax.nn.sigmoid(w_a0 + _mm3(_mm3(xa, w_a1), w_a2))
    if vres is None:
        v_first = v
    else:
        v0, v1, v2 = vres
        v = v + (v_first - v) * jax.nn.sigmoid(v0 + _mm3(_mm3(xv, v1), v2))

    def heads(t):
        return t.reshape(B, T, H, HEAD_DIM)

    r_h, k_h, v_h, a_h, w_h = heads(r), heads(k), heads(v), heads(a), heads(log_decay)
    kk = heads(k * k_k)
    kk = kk * lax.rsqrt(jnp.maximum(jnp.sum(kk * kk, axis=-1, keepdims=True), 1e-24))
    k_h = k_h * (1.0 + (a_h - 1.0) * k_a.reshape(H, HEAD_DIM))
    s_new, y = _wkv7(s0, r_h, w_h, k_h, v_h, kk, a_h)
    mean = jnp.mean(y, axis=-1, keepdims=True)
    var = jnp.mean(jnp.square(y - mean), axis=-1, keepdims=True)
    y = ((y - mean) * lax.rsqrt(var + GN_EPS) * ln_w.reshape(H, HEAD_DIM) + ln_b.reshape(H, HEAD_DIM))
    y = y + jnp.sum(r_h * k_h * r_k, axis=-1, keepdims=True) * v_h
    out = _mm3(y.reshape(B, T, D) * gate, w_o)
    return out, s_new, xn[:, -1], v_first


def _run_group(x, ple, ctx, prm):
    B, T, D = x.shape
    depth = prm['g_mix'].shape[0]
    past = 0 if ctx is None else ctx['page_table'].shape[1] * PAGE_SIZE
    pos = past + jnp.arange(T, dtype=jnp.int32)
    h = x
    v_first = None
    even_rows, odd_rows = [], []
    for i in range(depth):
        hn = _rmsnorm(h, prm['g_mix'][i])
        if i % 2 == 0:
            e = i // 2
            fq, fk, fv, logf, fg, dq, dk, dv, dg, iq, ik, iw = _even_project(
                hn, prm['w_in'][e], prm['b_forget'][e], pos)
            if ctx is None:
                o_f = _fox_prompt(fq, fk, fv, logf)
                o_d = _dsa_prompt(dq, dk, dv, iq, ik, iw)
            else:
                pt = ctx['page_table']
                o_f = _fox_sample(fq, fk, fv, logf, ctx['cache_k_fox'], ctx['cache_v_fox'],
                                  ctx['cache_logf_fox'], pt, e)
                o_d = _dsa_sample(dq, dk, dv, iq, ik, iw, ctx['cache_k_dsa'], ctx['cache_v_dsa'],
                                  ctx['cache_kidx_dsa'], pt, e)
            mixed = jnp.concatenate([o_f * fg, o_d * dg], axis=-1)
            h = h + _mm3(mixed, prm['w_out'][e])
            even_rows.append((fk, fv, logf, dk, dv, ik))
        else:
            o = i // 2
            if ctx is None:
                s0 = jnp.zeros((B, D // HEAD_DIM, HEAD_DIM, HEAD_DIM), x.dtype)
                x_prev = jnp.zeros((B, D), x.dtype)
            else:
                s0, x_prev = ctx['state_wkv'][o], ctx['state_shift'][o]
            vres = None if o == 0 else (prm['w_v0'][o - 1], prm['w_v1'][o - 1], prm['w_v2'][o - 1])
            out, s_new, shift_new, v_first = _rwkv_mix(
                hn, x_prev, s0, v_first, vres, prm['mu_rwkv'][o], prm['w_rkvg'][o], prm['w_o_rwkv'][o],
                prm['w_decay0'][o], prm['w_decay1'][o], prm['w_decay2'][o], prm['w_a0'][o], prm['w_a1'][o],
                prm['w_a2'][o], prm['k_k'][o], prm['k_a'][o], prm['r_k'][o], prm['ln_x_w'][o],
                prm['ln_x_b'][o])
            h = h + out
            odd_rows.append((s_new, shift_new))
        gate = jax.nn.sigmoid(_mm3(_rmsnorm(h, prm['g_ple'][i]), prm['w_ple_gate'][i]))
        h = h + gate * _mm3(ple[i], prm['w_ple_proj'][i])
    y = _rmsnorm(h, prm['g_final'])
    new = [jnp.stack([r[j] for r in even_rows], axis=1) for j in range(6)]
    new += [jnp.stack([r[j] for r in odd_rows], axis=0) for j in range(2)]
    return y, new


def kernel(x_prompt, x_sample, cache_k_fox, cache_v_fox, cache_logf_fox, cache_k_dsa, cache_v_dsa, cache_kidx_dsa, state_wkv, state_shift, page_table, p_prompt, p_sample, g_mix, w_in, b_forget, w_out, mu_rwkv, w_rkvg, w_o_rwkv, w_decay0, w_decay1, w_decay2, w_a0, w_a1, w_a2, w_v0, w_v1, w_v2, k_k, k_a, r_k, ln_x_w, ln_x_b, g_ple, w_ple_gate, w_ple_proj, g_final):
    prm = {'g_mix': g_mix, 'w_in': w_in, 'b_forget': b_forget, 'w_out': w_out, 'mu_rwkv': mu_rwkv,
           'w_rkvg': w_rkvg, 'w_o_rwkv': w_o_rwkv, 'w_decay0': w_decay0, 'w_decay1': w_decay1,
           'w_decay2': w_decay2, 'w_a0': w_a0, 'w_a1': w_a1, 'w_a2': w_a2, 'w_v0': w_v0, 'w_v1': w_v1,
           'w_v2': w_v2, 'k_k': k_k, 'k_a': k_a, 'r_k': r_k, 'ln_x_w': ln_x_w, 'ln_x_b': ln_x_b,
           'g_ple': g_ple, 'w_ple_gate': w_ple_gate, 'w_ple_proj': w_ple_proj, 'g_final': g_final}
    y_prompt, new_p = _run_group(x_prompt, p_prompt, None, prm)
    ctx = {'page_table': page_table, 'cache_k_fox': cache_k_fox, 'cache_v_fox': cache_v_fox,
           'cache_logf_fox': cache_logf_fox, 'cache_k_dsa': cache_k_dsa, 'cache_v_dsa': cache_v_dsa,
           'cache_kidx_dsa': cache_kidx_dsa, 'state_wkv': state_wkv, 'state_shift': state_shift}
    y_sample, new_s = _run_group(x_sample, p_sample, ctx, prm)
    return (y_prompt, y_sample) + tuple(new_p) + tuple(new_s)
```

```python
import functools

import jax
import jax.numpy as jnp
import numpy as np
from jax import lax
from jax.experimental import pallas as pl
from jax.experimental.pallas import tpu as pltpu

HEAD_DIM = 64
H_FOX = 8
H_DSA = 8
H_IDX = 8
D_IDX = 64
W_FOX = H_FOX * HEAD_DIM
W_DSA = H_DSA * HEAD_DIM
TOPK_MAX = 256
ROPE_THETA = 500000.0
ROPE_DIM = HEAD_DIM // 4
PAGE_SIZE = 128
RMS_EPS = 1e-6
GN_EPS = 64e-5
EVEN_SIZES = (W_FOX, W_FOX, W_FOX, H_FOX, W_FOX, W_DSA, W_DSA, W_DSA, W_DSA, H_IDX * D_IDX, D_IDX, H_IDX)
EVEN_CUTS = tuple(int(c) for c in np.cumsum(EVEN_SIZES)[:-1])

V7X_LANES = 128
V7X_VMEM_LIMIT_BYTES = 56 * 1024 * 1024
NEG_BIG = -1e30
INT_MIN = -2 ** 31

_BF16 = jnp.bfloat16
_F32 = jnp.float32


def _dot(a, b):
    return jnp.dot(a, b, preferred_element_type=_F32)


def _dot_nt(a, b):
    return lax.dot_general(a, b, (((1,), (1,)), ((), ())), preferred_element_type=_F32)


def _dot_tn(a, b):
    return lax.dot_general(a, b, (((0,), (0,)), ((), ())), preferred_element_type=_F32)


def _mm_kernel(x_ref, w_ref, o_ref):
    o_ref[...] = _dot(x_ref[...].astype(_BF16), w_ref[...])


def _even_proj_kernel(x_ref, w_ref, ws_ref, *o_refs, seg):
    xb = x_ref[...].astype(_BF16)
    for j, o_ref in enumerate(o_refs[:-1]):
        o_ref[...] = _dot(xb, w_ref[:, j * seg:(j + 1) * seg])
    o_refs[-1][...] = _dot(xb, ws_ref[...])


def _even_proj(hn, w_in):
    B, T, K = hn.shape
    M = B * T
    cols = jnp.split(w_in.astype(_BF16), EVEN_CUTS, axis=-1)
    wide = [0, 1, 2, 4, 5, 6, 7, 8, 9]
    narrow = [3, 10, 11]
    seg = W_FOX
    w_main = jnp.concatenate([cols[i] for i in wide], axis=-1)
    w_small = jnp.concatenate([cols[i] for i in narrow], axis=-1)
    n_small = w_small.shape[1]
    w_small = jnp.pad(w_small, ((0, 0), (0, V7X_LANES - n_small)))
    tm = min(512, M)
    outs = pl.pallas_call(
        functools.partial(_even_proj_kernel, seg=seg),
        grid=(M // tm,),
        in_specs=[pl.BlockSpec((tm, K), lambda i: (i, 0)),
                  pl.BlockSpec((K, seg * len(wide)), lambda i: (0, 0)),
                  pl.BlockSpec((K, V7X_LANES), lambda i: (0, 0))],
        out_specs=[pl.BlockSpec((tm, seg), lambda i: (i, 0))] * len(wide)
        + [pl.BlockSpec((tm, V7X_LANES), lambda i: (i, 0))],
        out_shape=[jax.ShapeDtypeStruct((M, seg), _F32)] * len(wide)
        + [jax.ShapeDtypeStruct((M, V7X_LANES), _F32)],
        compiler_params=pltpu.CompilerParams(
            dimension_semantics=("parallel",),
            vmem_limit_bytes=V7X_VMEM_LIMIT_BYTES),
        name="even_in_proj",
    )(hn.reshape(M, K), w_main, w_small)
    groups = [None] * len(EVEN_SIZES)
    for i, o in zip(wide, outs[:-1]):
        groups[i] = o.reshape(B, T, seg)
    off = 0
    for i in narrow:
        groups[i] = outs[-1][:, off:off + EVEN_SIZES[i]].reshape(B, T, EVEN_SIZES[i])
        off += EVEN_SIZES[i]
    return groups


def _mm(x, w):
    M, K = x.shape
    N = w.shape[1]
    n_pad = (-N) % V7X_LANES
    wb = w.astype(_BF16)
    if n_pad:
        wb = jnp.pad(wb, ((0, 0), (0, n_pad)))
    Np = N + n_pad
    tm = min(512, M)
    tn = Np
    for cand in (1024, 768, 512, 384, 256, 128):
        if Np % cand == 0:
            tn = cand
            break
    assert M % tm == 0
    out = pl.pallas_call(
        _mm_kernel,
        grid=(M // tm, Np // tn),
        in_specs=[pl.BlockSpec((tm, K), lambda i, j: (i, 0)),
                  pl.BlockSpec((K, tn), lambda i, j: (0, j))],
        out_specs=pl.BlockSpec((tm, tn), lambda i, j: (i, j)),
        out_shape=jax.ShapeDtypeStruct((M, Np), _F32),
        compiler_params=pltpu.CompilerParams(
            dimension_semantics=("parallel", "parallel"),
            vmem_limit_bytes=V7X_VMEM_LIMIT_BYTES),
        name="proj_matmul",
    )(x, wb)
    return out[:, :N] if n_pad else out


def _mm3(x, w):
    B, T, K = x.shape
    return _mm(x.reshape(B * T, K), w).reshape(B, T, w.shape[1])


def _causal_pairs(S, tq, tk):
    qi, ki = [], []
    for q in range(S // tq):
        for k in range((q * tq + tq - 1) // tk + 1):
            qi.append(q)
            ki.append(k)
    return jnp.asarray(qi, jnp.int32), jnp.asarray(ki, jnp.int32)


LOG2E = 1.4426950408889634


def _online_softmax_step(s, v, m_sc, l_sc, acc_sc, h):
    m_old = m_sc[h]
    m_new = jnp.maximum(m_old, jnp.max(s, axis=-1, keepdims=True))
    alpha = jnp.exp2(m_old - m_new)
    p = jnp.exp2(s - m_new)
    l_sc[h] = alpha * l_sc[h] + jnp.sum(p, axis=-1, keepdims=True)
    acc_sc[h] = alpha * acc_sc[h] + _dot(p.astype(_BF16), v)
    m_sc[h] = m_new


def _attend_heads(q_ref, k_ref, v_ref, m_sc, l_sc, acc_sc, nh, keep, bias=None):
    s_next = _dot_nt(q_ref[0, 0], k_ref[0, 0])
    for h in range(nh):
        s = s_next
        if h + 1 < nh:
            s_next = _dot_nt(q_ref[0, h + 1], k_ref[0, h + 1])
        if bias is not None:
            s = s - bias[0, h]
        if keep is not None:
            s = jnp.where(keep, s, NEG_BIG)
        _online_softmax_step(s, v_ref[0, h], m_sc, l_sc, acc_sc, h)


def _attn_init(ki, m_sc, l_sc, acc_sc):
    @pl.when(ki == 0)
    def _():
        m_sc[...] = jnp.full(m_sc.shape, NEG_BIG, _F32)
        l_sc[...] = jnp.zeros(l_sc.shape, _F32)
        acc_sc[...] = jnp.zeros(acc_sc.shape, _F32)


def _attn_finish(ki, k_last, o_ref, l_sc, acc_sc, nh):
    @pl.when(ki == k_last)
    def _():
        for h in range(nh):
            o_ref[0, h] = acc_sc[h] / l_sc[h]


def _fox_attn_kernel(qi_tab, ki_tab, q_ref, k_ref, v_ref, ck_ref, o_ref, m_sc, l_sc, acc_sc,
                     *, tq, tk, nh):
    p = pl.program_id(1)
    qi = qi_tab[p]
    ki = ki_tab[p]
    k_last = (qi * tq + tq - 1) // tk
    _attn_init(ki, m_sc, l_sc, acc_sc)
    crosses_diagonal = ki * tk + tk - 1 > qi * tq

    @pl.when(crosses_diagonal)
    def _():
        rows = qi * tq + lax.broadcasted_iota(jnp.int32, (tq, tk), 0)
        cols = ki * tk + lax.broadcasted_iota(jnp.int32, (tq, tk), 1)
        _attend_heads(q_ref, k_ref, v_ref, m_sc, l_sc, acc_sc, nh, cols <= rows, ck_ref)

    @pl.when(jnp.logical_not(crosses_diagonal))
    def _():
        _attend_heads(q_ref, k_ref, v_ref, m_sc, l_sc, acc_sc, nh, None, ck_ref)

    _attn_finish(ki, k_last, o_ref, l_sc, acc_sc, nh)


def _fox_prompt(q, k, v, logf, tq=512, tk=1024):
    B, S, H, dh = q.shape
    tq, tk = min(tq, S), min(tk, S)
    ck = jnp.swapaxes(jnp.cumsum(logf, axis=1) * LOG2E, 1, 2)[:, :, None, :]
    qh = jnp.swapaxes(q * (HEAD_DIM ** -0.5 * LOG2E), 1, 2).astype(_BF16)
    kh = jnp.swapaxes(k, 1, 2).astype(_BF16)
    vh = jnp.swapaxes(v, 1, 2).astype(_BF16)
    qi_tab, ki_tab = _causal_pairs(S, tq, tk)
    grid_spec = pltpu.PrefetchScalarGridSpec(
        num_scalar_prefetch=2,
        grid=(B, int(qi_tab.shape[0])),
        in_specs=[
            pl.BlockSpec((1, H, tq, dh), lambda b, p, qt, kt: (b, 0, qt[p], 0)),
            pl.BlockSpec((1, H, tk, dh), lambda b, p, qt, kt: (b, 0, kt[p], 0)),
            pl.BlockSpec((1, H, tk, dh), lambda b, p, qt, kt: (b, 0, kt[p], 0)),
            pl.BlockSpec((1, H, 1, tk), lambda b, p, qt, kt: (b, 0, 0, kt[p])),
        ],
        out_specs=pl.BlockSpec((1, H, tq, dh), lambda b, p, qt, kt: (b, 0, qt[p], 0)),
        scratch_shapes=[pltpu.VMEM((H, tq, 1), _F32), pltpu.VMEM((H, tq, 1), _F32),
                        pltpu.VMEM((H, tq, dh), _F32)],
    )
    out = pl.pallas_call(
        functools.partial(_fox_attn_kernel, tq=tq, tk=tk, nh=H),
        grid_spec=grid_spec,
        out_shape=jax.ShapeDtypeStruct((B, H, S, dh), _F32),
        compiler_params=pltpu.CompilerParams(
            dimension_semantics=("parallel", "arbitrary"),
            vmem_limit_bytes=V7X_VMEM_LIMIT_BYTES),
        name="fox_prompt_attn",
    )(qi_tab, ki_tab, qh, kh, vh, ck)
    return jnp.swapaxes(out, 1, 2).reshape(B, S, H * dh)


def _sortable_key(x):
    b = pltpu.bitcast(x, jnp.int32)
    return b ^ ((b >> 31) & jnp.int32(0x7FFFFFFF))


def _topk_rank_threshold(count, nrows, topk):
    kf = float(topk)

    def bit_step(i, v):
        cand = v + jnp.left_shift(jnp.int32(1), 31 - i)
        cnt = count(lambda kk, cols: kk >= cand)
        return jnp.where(cnt >= kf, cand, v)

    v = lax.fori_loop(0, 32, bit_step, jnp.full((nrows, 1), INT_MIN, jnp.int32))
    n_gt = count(lambda kk, cols: kk > v)
    n_ge = count(lambda kk, cols: kk >= v)
    need = kf - n_gt
    has_thr = v != jnp.int32(INT_MIN)
    ties = jnp.max(jnp.where(has_thr & (n_ge > kf), 1.0, 0.0)) > 0.0

    def tie_search(_):
        def idx_step(i, m):
            cand = m + jnp.left_shift(jnp.int32(1), 30 - i)
            cnt = count(lambda kk, cols: (kk == v) & (cols < cand))
            return jnp.where(cnt < need, cand, m)
        return lax.fori_loop(0, 31, idx_step, jnp.zeros((nrows, 1), jnp.int32))

    m_last = lax.cond(ties, tie_search, lambda _: jnp.full((nrows, 1), 2 ** 31 - 1, jnp.int32), 0)
    return v, m_last


def _topk_keep(kk, cols, v, m_last):
    return ((kk > v) | ((kk == v) & (cols <= m_last))) & (kk != jnp.int32(INT_MIN))


def _dsa_select_kernel(iq_ref, iw_ref, ikt_ref, mask_ref, key_sc, *, tq, tk, nk, nh, topk):
    qi = pl.program_id(1)
    nkc = (qi * tq + tq - 1) // tk + 1
    rows = qi * tq + lax.broadcasted_iota(jnp.int32, (tq, tk), 0)
    col0 = lax.broadcasted_iota(jnp.int32, (tq, tk), 1)
    iw = iw_ref[0]

    def score_chunk(c, carry):
        kt = ikt_ref[0, c]
        acc = jnp.zeros((tq, tk), _F32)
        for h in range(nh):
            d = jnp.dot(iq_ref[0, h], kt, preferred_element_type=_F32)
            acc = acc + jnp.maximum(d, 0.0) * iw[:, h:h + 1]
        key = _sortable_key(acc)
        key_sc[c] = jnp.where(c * tk + col0 <= rows, key, jnp.int32(INT_MIN))
        return carry

    lax.fori_loop(0, nkc, score_chunk, 0)

    def count(pred):
        def body(c, part):
            hit = jnp.where(pred(key_sc[c], c * tk + col0), 1.0, 0.0)
            for j in range(tk // V7X_LANES):
                part = part + hit[:, j * V7X_LANES:(j + 1) * V7X_LANES]
            return part
        part = lax.fori_loop(0, nkc, body, jnp.zeros((tq, V7X_LANES), _F32))
        return jnp.sum(part, axis=-1, keepdims=True)

    v, m_last = _topk_rank_threshold(count, tq, topk)

    def write_chunk(c, carry):
        sel = _topk_keep(key_sc[c], c * tk + col0, v, m_last)
        mask_ref[0, 0, c] = sel.astype(jnp.int8)
        return carry

    lax.fori_loop(0, nkc, write_chunk, 0)

    def zero_chunk(c, carry):
        mask_ref[0, 0, c] = jnp.zeros((tq, tk), jnp.int8)
        return carry

    lax.fori_loop(nkc, nk, zero_chunk, 0)


def _dsa_select(iq, ik, iw, topk, tq=256, tk=1024):
    B, S, Hi, Di = iq.shape
    tq, tk = min(tq, S), min(tk, S)
    nq, nk = S // tq, S // tk
    iqh = jnp.swapaxes(iq, 1, 2).astype(_BF16)
    ikt = jnp.swapaxes(ik.reshape(B, nk, tk, Di), 2, 3).astype(_BF16)
    iws = (iw * (D_IDX ** -0.5)).astype(_F32)
    return pl.pallas_call(
        functools.partial(_dsa_select_kernel, tq=tq, tk=tk, nk=nk, nh=Hi, topk=topk),
        grid=(B, nq),
        in_specs=[pl.BlockSpec((1, Hi, tq, Di), lambda b, q: (b, 0, q, 0)),
                  pl.BlockSpec((1, tq, Hi), lambda b, q: (b, q, 0)),
                  pl.BlockSpec((1, nk, Di, tk), lambda b, q: (b, 0, 0, 0))],
        out_specs=pl.BlockSpec((1, 1, nk, tq, tk), lambda b, q: (b, q, 0, 0, 0)),
        out_shape=jax.ShapeDtypeStruct((B, nq, nk, tq, tk), jnp.int8),
        scratch_shapes=[pltpu.VMEM((nk, tq, tk), jnp.int32)],
        compiler_params=pltpu.CompilerParams(
            dimension_semantics=("parallel", "arbitrary"),
            vmem_limit_bytes=V7X_VMEM_LIMIT_BYTES),
        name="dsa_topk_select",
    )(iqh, iws, ikt)


def _dsa_attn_kernel(qi_tab, ki_tab, q_ref, k_ref, v_ref, mask_ref, o_ref, m_sc, l_sc, acc_sc,
                     *, tq, tk, nh):
    p = pl.program_id(1)
    qi = qi_tab[p]
    ki = ki_tab[p]
    k_last = (qi * tq + tq - 1) // tk
    _attn_init(ki, m_sc, l_sc, acc_sc)
    keep = mask_ref[0, :, 0].reshape(tq, tk) != 0
    _attend_heads(q_ref, k_ref, v_ref, m_sc, l_sc, acc_sc, nh, keep)
    _attn_finish(ki, k_last, o_ref, l_sc, acc_sc, nh)


def _dsa_prompt(q, k, v, iq, ik, iw, tq=512, tk=1024, tq_sel=128):
    B, S, H, dh = q.shape
    topk = min(TOPK_MAX, S // 4)
    tq, tk, tq_sel = min(tq, S), min(tk, S), min(tq_sel, S)
    mask = _dsa_select(iq, ik, iw, topk, tq=tq_sel, tk=tk)
    r = tq // tq_sel
    qh = jnp.swapaxes(q * (HEAD_DIM ** -0.5 * LOG2E), 1, 2).astype(_BF16)
    kh = jnp.swapaxes(k, 1, 2).astype(_BF16)
    vh = jnp.swapaxes(v, 1, 2).astype(_BF16)
    qi_tab, ki_tab = _causal_pairs(S, tq, tk)
    grid_spec = pltpu.PrefetchScalarGridSpec(
        num_scalar_prefetch=2,
        grid=(B, int(qi_tab.shape[0])),
        in_specs=[
            pl.BlockSpec((1, H, tq, dh), lambda b, p, qt, kt: (b, 0, qt[p], 0)),
            pl.BlockSpec((1, H, tk, dh), lambda b, p, qt, kt: (b, 0, kt[p], 0)),
            pl.BlockSpec((1, H, tk, dh), lambda b, p, qt, kt: (b, 0, kt[p], 0)),
            pl.BlockSpec((1, r, 1, tq_sel, tk), lambda b, p, qt, kt: (b, qt[p], kt[p], 0, 0)),
        ],
        out_specs=pl.BlockSpec((1, H, tq, dh), lambda b, p, qt, kt: (b, 0, qt[p], 0)),
        scratch_shapes=[pltpu.VMEM((H, tq, 1), _F32), pltpu.VMEM((H, tq, 1), _F32),
                        pltpu.VMEM((H, tq, dh), _F32)],
    )
    out = pl.pallas_call(
        functools.partial(_dsa_attn_kernel, tq=tq, tk=tk, nh=H),
        grid_spec=grid_spec,
        out_shape=jax.ShapeDtypeStruct((B, H, S, dh), _F32),
        compiler_params=pltpu.CompilerParams(
            dimension_semantics=("parallel", "arbitrary"),
            vmem_limit_bytes=V7X_VMEM_LIMIT_BYTES),
        name="dsa_prompt_attn",
    )(qi_tab, ki_tab, qh, kh, vh, mask)
    return jnp.swapaxes(out, 1, 2).reshape(B, S, H * dh)


def _split2(x):
    hi = x.astype(_BF16)
    lo = (x - hi.astype(_F32)).astype(_BF16)
    return hi, lo


def _dot3(a, b):
    ah, al = a
    bh, bl = b
    return _dot(ah, bh) + _dot(ah, bl) + _dot(al, bh)


def _wkv_chunk_kernel(r_ref, lw_ref, k_ref, v_ref, al_ref, be_ref, s0_ref, y_ref, sT_ref, s_sc,
                      *, L, n, npair, nchunk):
    t_blk = pl.program_id(2)
    L2, n2 = 2 * L, 2 * n

    @pl.when(t_blk == 0)
    def _():
        s_sc[...] = s0_ref[0]

    ri = lax.broadcasted_iota(jnp.int32, (L2, L2), 0)
    ci = lax.broadcasted_iota(jnp.int32, (L2, L2), 1)
    same_head = (ri // L) == (ci // L)
    strict = same_head & (ci < ri)
    incl = same_head & (ci <= ri)
    bs = min(16, L)
    diag_blk = (ri // bs) == (ci // bs)
    eye = (ri == ci).astype(_F32)
    cum_tri = (lax.broadcasted_iota(jnp.int32, (L, L), 0)
               >= lax.broadcasted_iota(jnp.int32, (L, L), 1)).astype(_BF16)
    sr = lax.broadcasted_iota(jnp.int32, (n2, n2), 0)
    sc = lax.broadcasted_iota(jnp.int32, (n2, n2), 1)
    state_blk = (sr // n) == (sc // n)
    lane_a = lax.broadcasted_iota(jnp.int32, (L, n2), 1) < n

    def expand(x):
        return jnp.concatenate([jnp.where(lane_a, x, 0.0), jnp.where(lane_a, 0.0, x)], axis=0)

    def compact(x):
        return x[:L] + x[L:]

    def each(fn, *cols):
        return [fn(*args) for args in zip(*cols)]

    def chunk(c, carry):
        rows = pl.ds(pl.multiple_of(c * L, L), L)
        lane_sl = [slice(p * n2, (p + 1) * n2) for p in range(npair)]
        lw = [lw_ref[0, rows, ls] for ls in lane_sl]
        r = [r_ref[0, rows, ls] for ls in lane_sl]
        k = [k_ref[0, rows, ls] for ls in lane_sl]
        v = [v_ref[0, rows, ls] for ls in lane_sl]
        al = [al_ref[0, rows, ls] for ls in lane_sl]
        be = [be_ref[0, rows, ls] for ls in lane_sl]

        def cumsum(x):
            hi = x.astype(_BF16)
            r1 = x - hi.astype(_F32)
            mid = r1.astype(_BF16)
            lo = (r1 - mid.astype(_F32)).astype(_BF16)
            return _dot(cum_tri, hi) + _dot(cum_tri, mid) + _dot(cum_tri, lo)

        g = each(cumsum, lw)
        g_last = each(lambda x: x[L - 1:L, :], g)
        a_t = each(lambda x, gg, l: x * jnp.exp(gg - l), al, g, lw)
        r_t = each(lambda x, gg: x * jnp.exp(gg), r, g)
        eng = each(lambda gg: jnp.exp(-gg), g)
        pa_pr = each(lambda x, y: jnp.concatenate([expand(x), expand(y)], axis=0).astype(_BF16), a_t, r_t)
        pb = each(lambda x, e: expand(x * e).astype(_BF16), be, eng)
        pk = each(lambda x, e: expand(x * e).astype(_BF16), k, eng)
        pv = each(lambda x: expand(x).astype(_BF16), v)
        xb = each(_dot_nt, pa_pr, pb)
        xk = each(_dot_nt, pa_pr, pk)
        m_ab = each(lambda x: jnp.where(strict, x[:L2], 0.0), xb)
        m_ak = each(lambda x: jnp.where(strict, x[:L2], 0.0).astype(_BF16), xk)
        n_rb = each(lambda x: jnp.where(incl, x[L2:], 0.0).astype(_BF16), xb)
        n_rk = each(lambda x: jnp.where(incl, x[L2:], 0.0).astype(_BF16), xk)
        d1 = each(lambda x: jnp.where(diag_blk, x, 0.0), m_ab)
        e_s = each(lambda x, d: _split2(x - d), m_ab, d1)
        d1s = each(_split2, d1)
        d2 = each(_dot3, d1s, d1s)
        d2s = each(_split2, d2)
        d4 = each(_dot3, d2s, d2s)
        d4s = each(_split2, d4)
        d8 = each(_dot3, d4s, d4s)
        p12 = each(lambda x, y: _dot3(_split2(eye + x), _split2(eye + y)), d1, d2)
        p48 = each(lambda x, y: _dot3(_split2(eye + x), _split2(eye + y)), d4, d8)
        xs = each(lambda x, y: _split2(_dot3(_split2(x), _split2(y))), p12, p48)
        nn = each(_dot3, xs, e_s)
        nns = each(_split2, nn)
        nn2 = each(_dot3, nns, nns)
        q12 = each(lambda x, y: _dot3(_split2(eye + x), _split2(eye + y)), nn, nn2)
        tinv = each(lambda x, y: _dot3(_split2(x), y).astype(_BF16), q12, xs)
        s = [s_sc[p] for p in range(npair)]
        ars = each(lambda x, y, st: _dot_nt(jnp.concatenate([x, y], axis=0).astype(_BF16), st.astype(_BF16)),
                   a_t, r_t, s)
        rhs = each(lambda x, m, vv: (expand(x[:L]) + _dot(m, vv)).astype(_BF16), ars, m_ak, pv)
        u_e = each(_dot, tinv, rhs)
        y_e = each(lambda x, nb, u, nk, vv: expand(x[L:]) + _dot(nb, u.astype(_BF16)) + _dot(nk, vv),
                   ars, n_rb, u_e, n_rk, pv)
        for ls, ye in zip(lane_sl, y_e):
            y_ref[0, rows, ls] = compact(ye)
        dec = each(lambda gl, gg: jnp.exp(gl - gg), g_last, g)
        uv = each(lambda u, vv: jnp.concatenate([compact(u), vv], axis=0).astype(_BF16), u_e, v)
        bk = each(lambda x, y, d: jnp.concatenate([x * d, y * d], axis=0).astype(_BF16), be, k, dec)
        upd = each(_dot_tn, uv, bk)
        for p in range(npair):
            s_sc[p] = s[p] * jnp.exp(g_last[p]) + jnp.where(state_blk, upd[p], 0.0)
        return carry

    lax.fori_loop(0, nchunk, chunk, 0)

    @pl.when(t_blk == pl.num_programs(2) - 1)
    def _():
        sT_ref[0] = s_sc[...]


def _wkv7(s0, r, lw, k, v, kk, a, L=64, tc=256, npair=8):
    B, T, H, N = r.shape
    L, tc = min(L, T), min(tc, T)
    npair = min(npair, H // 2)
    D = H * N
    flat = lambda x: x.reshape(B, T, D)
    s0p = s0.reshape(B, H // 2, 2, N, N)
    z = jnp.zeros_like(s0p[:, :, 0])
    s0bd = jnp.concatenate([jnp.concatenate([s0p[:, :, 0], z], axis=-1),
                            jnp.concatenate([z, s0p[:, :, 1]], axis=-1)], axis=-2)
    seq_spec = pl.BlockSpec((1, tc, npair * 2 * N), lambda b, g, t: (b, t, g))
    st_spec = pl.BlockSpec((1, npair, 2 * N, 2 * N), lambda b, g, t: (b, g, 0, 0))
    y, sT = pl.pallas_call(
        functools.partial(_wkv_chunk_kernel, L=L, n=N, npair=npair, nchunk=tc // L),
        grid=(B, H // (2 * npair), T // tc),
        in_specs=[seq_spec] * 6 + [st_spec],
        out_specs=[seq_spec, st_spec],
        out_shape=[jax.ShapeDtypeStruct((B, T, D), _F32),
                   jax.ShapeDtypeStruct((B, H // 2, 2 * N, 2 * N), _F32)],
        scratch_shapes=[pltpu.VMEM((npair, 2 * N, 2 * N), _F32)],
        compiler_params=pltpu.CompilerParams(
            dimension_semantics=("parallel", "parallel", "arbitrary"),
            vmem_limit_bytes=V7X_VMEM_LIMIT_BYTES),
        name="wkv7_chunked",
    )(flat(r), flat(lw), flat(k), flat(v), flat(-kk), flat(kk * a), s0bd)
    sT = jnp.stack([sT[:, :, :N, :N], sT[:, :, N:, N:]], axis=2).reshape(B, H, N, N)
    return sT, y.reshape(B, T, H, N)


def _rmsnorm(x, g):
    y = x * lax.rsqrt(jnp.mean(x * x, axis=-1, keepdims=True) + RMS_EPS)
    return y * g


def _partial_rope(x, pos):
    half = ROPE_DIM // 2
    inv = ROPE_THETA ** (-2.0 * jnp.arange(half, dtype=_F32) / ROPE_DIM)
    ang = pos.astype(_F32)[:, None] * inv
    cos, sin = jnp.cos(ang)[:, None, :], jnp.sin(ang)[:, None, :]
    x1, x2 = x[..., :half], x[..., half:ROPE_DIM]
    return jnp.concatenate([x1 * cos - x2 * sin, x2 * cos + x1 * sin, x[..., ROPE_DIM:]], axis=-1)


def _even_project(hn, w_in, b_forget, pos):
    B, T, _ = hn.shape
    fq, fk, fv, ff, fg, dq, dk, dv, dg, iq, ik, iw = _even_proj(hn, w_in)

    def heads(t, h):
        return t.reshape(B, T, h, -1)

    logf = jax.nn.log_sigmoid(ff + b_forget)
    dq = _partial_rope(heads(dq, H_DSA), pos)
    dk = _partial_rope(heads(dk, H_DSA), pos)
    iq = _partial_rope(heads(iq, H_IDX), pos)
    ik = _partial_rope(ik[:, :, None, :], pos)[:, :, 0]
    iw = iw * (H_IDX ** -0.5)
    return (heads(fq, H_FOX), heads(fk, H_FOX), heads(fv, H_FOX), logf, jax.nn.silu(fg),
            dq, dk, heads(dv, H_DSA), jax.nn.silu(dg), iq, ik, iw)


PAGES_PER_STEP = 8
NEW_ROWS_PAD = 16


def _paged_attn_kernel(pt_ref, q_ref, roff_ref, bias_ref, *refs, pg, page):
    k_refs, v_refs = refs[:pg], refs[pg:2 * pg]
    kn_ref, vn_ref, bn_ref, o_ref, m_sc, l_sc, acc_sc = refs[2 * pg:]
    j = pl.program_id(1)

    @pl.when(j == 0)
    def _():
        m_sc[...] = jnp.full(m_sc.shape, NEG_BIG, _F32)
        l_sc[...] = jnp.zeros(l_sc.shape, _F32)
        acc_sc[...] = jnp.zeros(acc_sc.shape, _F32)

    q = q_ref[0]
    roff = roff_ref[0]
    nrep = q.shape[0] // bias_ref.shape[1]

    def update(scores, values, weigh):
        m_old = m_sc[...]
        m_new = m_old
        for s in scores:
            m_new = jnp.maximum(m_new, jnp.max(s, axis=-1, keepdims=True))
        alpha = jnp.exp(m_old - m_new)
        l_new = alpha * l_sc[...]
        acc = alpha * acc_sc[...]
        for s, v in zip(scores, values):
            p = jnp.exp(s - m_new)
            l_new = l_new + jnp.sum(p, axis=-1, keepdims=True)
            acc = acc + weigh(p.astype(_BF16), v)
        l_sc[...] = l_new
        acc_sc[...] = acc
        m_sc[...] = m_new

    def page_t(ref):
        return ref[0, 0].reshape(q.shape[1], page).astype(_BF16)

    scores = []
    for i in range(pg):
        b = bias_ref[0, :, i * page:(i + 1) * page]
        scores.append(_dot(q, page_t(k_refs[i])) + roff + jnp.concatenate([b] * nrep, axis=0))
    update(scores, [page_t(v_refs[i]) for i in range(pg)], _dot_nt)

    @pl.when(j == pl.num_programs(1) - 1)
    def _():
        update([_dot_nt(q, kn_ref[0].astype(_BF16)) + bn_ref[0]], [vn_ref[0].astype(_BF16)], _dot)
        o_ref[0] = acc_sc[...] / l_sc[...]


def _paged_attention(qbd, roff, bias_t, cache_k, cache_v, page_table, layer, k_new, v_new, bias_new):
    B, R, W = qbd.shape
    nh, dh = cache_k.shape[3:]
    cache_k = jnp.transpose(cache_k, (0, 1, 3, 4, 2))
    cache_v = jnp.transpose(cache_v, (0, 1, 3, 4, 2))
    npages = page_table.shape[1]
    pg = min(PAGES_PER_STEP, npages)
    assert npages % pg == 0
    tn = k_new.shape[1]
    tile_rows = bias_t.shape[1]

    def page_spec(i):
        return pl.BlockSpec((1, 1, nh, dh, PAGE_SIZE),
                            lambda b, j, pt, i=i: (pt[b * npages + j * pg + i], layer, 0, 0, 0))

    grid_spec = pltpu.PrefetchScalarGridSpec(
        num_scalar_prefetch=1,
        grid=(B, npages // pg),
        in_specs=[pl.BlockSpec((1, R, W), lambda b, j, pt: (b, 0, 0)),
                  pl.BlockSpec((1, R, 1), lambda b, j, pt: (b, 0, 0)),
                  pl.BlockSpec((1, tile_rows, pg * PAGE_SIZE), lambda b, j, pt: (b, 0, j))]
        + [page_spec(i) for i in range(pg)] + [page_spec(i) for i in range(pg)]
        + [pl.BlockSpec((1, tn, W), lambda b, j, pt: (b, 0, 0)),
           pl.BlockSpec((1, tn, W), lambda b, j, pt: (b, 0, 0)),
           pl.BlockSpec((1, R, tn), lambda b, j, pt: (b, 0, 0))],
        out_specs=pl.BlockSpec((1, R, W), lambda b, j, pt: (b, 0, 0)),
        scratch_shapes=[pltpu.VMEM((R, 1), _F32), pltpu.VMEM((R, 1), _F32), pltpu.VMEM((R, W), _F32)],
    )
    return pl.pallas_call(
        functools.partial(_paged_attn_kernel, pg=pg, page=PAGE_SIZE),
        grid_spec=grid_spec,
        out_shape=jax.ShapeDtypeStruct((B, R, W), _F32),
        compiler_params=pltpu.CompilerParams(
            dimension_semantics=("parallel", "arbitrary"),
            vmem_limit_bytes=V7X_VMEM_LIMIT_BYTES),
        name="paged_decode_attn",
    )(page_table.reshape(-1), qbd, roff, bias_t, *([cache_k] * pg), *([cache_v] * pg), k_new, v_new, bias_new)


def _pad_rows(x, rows, value=0.0):
    return jnp.pad(x, ((0, 0), (0, rows - x.shape[1]), (0, 0)), constant_values=value)


def _fox_sample(q, k, v, logf, cache_k, cache_v, cache_logf, page_table, layer):
    B, T, H, dh = q.shape
    W = H * dh
    logf_past = cache_logf[page_table, layer].reshape(B, -1, H)
    c_past = jnp.cumsum(logf_past, axis=1)
    c_new = c_past[:, -1:] + jnp.cumsum(logf, axis=1)
    eye = jnp.eye(H, dtype=_F32)
    qs = q * (HEAD_DIM ** -0.5)
    qbd = (qs[:, :, :, None, :] * eye[None, None, :, :, None]).reshape(B, T * H, W).astype(_BF16)
    roff = c_new.reshape(B, T * H, 1)
    bias_t = -jnp.swapaxes(c_past, 1, 2)
    tri = jnp.arange(T)[None, :] <= jnp.arange(T)[:, None]
    bn = c_new[:, :, None, :] - c_new[:, None, :, :]
    bn = jnp.where(tri[None, :, :, None], bn, NEG_BIG)
    bn = jnp.swapaxes(bn, 2, 3).reshape(B, T * H, T)
    bn = jnp.pad(bn, ((0, 0), (0, 0), (0, NEW_ROWS_PAD - T)), constant_values=NEG_BIG)
    out = _paged_attention(qbd, roff, bias_t, cache_k, cache_v, page_table, layer,
                           _pad_rows(k.reshape(B, T, W), NEW_ROWS_PAD),
                           _pad_rows(v.reshape(B, T, W), NEW_ROWS_PAD), bn)
    out = out.reshape(B, T, H, H, dh)
    return jnp.stack([out[:, :, h, h] for h in range(H)], axis=2).reshape(B, T, W)


def _dsa_sample_select_kernel(pt_ref, x_ref, w_ref, kn_ref, *refs, npages, page, nq, nh, topk):
    kp_refs = refs[:npages]
    bias_ref, key_sc = refs[npages:]
    i = pl.program_id(1)
    x = x_ref[0]
    w = w_ref[0]
    r0 = pl.multiple_of(i * nq, nq)

    def fold(d):
        t = jnp.maximum(d, 0.0) * w
        acc = t[0:nq]
        for h in range(1, nh):
            acc = acc + t[h * nq:(h + 1) * nq]
        return acc

    for p in range(npages):
        d = _dot(x, kp_refs[p][0, 0].astype(_BF16))
        key_sc[pl.ds(r0, nq), p * page:(p + 1) * page] = _sortable_key(fold(d))
    dn = _dot_nt(x, kn_ref[0].astype(_BF16))
    qrow = lax.broadcasted_iota(jnp.int32, (nq, page), 0)
    col = lax.broadcasted_iota(jnp.int32, (nq, page), 1)
    key_sc[pl.ds(r0, nq), npages * page:(npages + 1) * page] = jnp.where(
        col <= qrow, _sortable_key(fold(dn)), jnp.int32(INT_MIN))

    @pl.when(i == pl.num_programs(1) - 1)
    def _():
        nrows, ncols = key_sc.shape
        cols = lax.broadcasted_iota(jnp.int32, (nrows, ncols), 1)

        def count(pred):
            return jnp.sum(jnp.where(pred(key_sc[...], cols), 1.0, 0.0), axis=-1, keepdims=True)

        v, m_last = _topk_rank_threshold(count, nrows, topk)
        bias_ref[0] = jnp.where(_topk_keep(key_sc[...], cols, v, m_last), 0.0, NEG_BIG)


def _dsa_sample_select(iq, ik, iw, cache_kidx, page_table, layer, topk, gb=8):
    B, T, Hi, Di = iq.shape
    npages = page_table.shape[1]
    gb = min(gb, B)
    assert B % gb == 0
    ncols = (npages + 1) * PAGE_SIZE
    x = jnp.swapaxes(iq, 1, 2).reshape(B, Hi * T, Di).astype(_BF16)
    w = jnp.swapaxes(iw * (D_IDX ** -0.5), 1, 2).reshape(B, Hi * T, 1)
    kn = _pad_rows(ik, PAGE_SIZE)

    def page_spec(p):
        return pl.BlockSpec((1, 1, Di, PAGE_SIZE),
                            lambda g, i, pt, p=p: (pt[(g * gb + i) * npages + p], layer, 0, 0))

    grid_spec = pltpu.PrefetchScalarGridSpec(
        num_scalar_prefetch=1,
        grid=(B // gb, gb),
        in_specs=[pl.BlockSpec((1, Hi * T, Di), lambda g, i, pt: (g * gb + i, 0, 0)),
                  pl.BlockSpec((1, Hi * T, 1), lambda g, i, pt: (g * gb + i, 0, 0)),
                  pl.BlockSpec((1, PAGE_SIZE, Di), lambda g, i, pt: (g * gb + i, 0, 0))]
        + [page_spec(p) for p in range(npages)],
        out_specs=pl.BlockSpec((1, gb * T, ncols), lambda g, i, pt: (g, 0, 0)),
        scratch_shapes=[pltpu.VMEM((gb * T, ncols), jnp.int32)],
    )
    bias = pl.pallas_call(
        functools.partial(_dsa_sample_select_kernel, npages=npages, page=PAGE_SIZE, nq=T, nh=Hi, topk=topk),
        grid_spec=grid_spec,
        out_shape=jax.ShapeDtypeStruct((B // gb, gb * T, ncols), _F32),
        compiler_params=pltpu.CompilerParams(
            dimension_semantics=("parallel", "arbitrary"),
            vmem_limit_bytes=V7X_VMEM_LIMIT_BYTES),
        name="dsa_sample_select",
    )(page_table.reshape(-1), x, w, kn, *([jnp.swapaxes(cache_kidx, 2, 3)] * npages))
    return bias.reshape(B, T, ncols)


def _dsa_sample(q, k, v, iq, ik, iw, cache_k, cache_v, cache_kidx, page_table, layer):
    B, T, H, dh = q.shape
    W = H * dh
    P = page_table.shape[1] * PAGE_SIZE
    topk = min(TOPK_MAX, (P + T) // 4)
    sel = _dsa_sample_select(iq, ik, iw, cache_kidx, page_table, layer, topk)
    eye = jnp.eye(H, dtype=_F32)
    qh = jnp.swapaxes(q * (HEAD_DIM ** -0.5), 1, 2)
    qbd = (qh[:, :, :, None, :] * eye[None, :, None, :, None]).reshape(B, H * T, W).astype(_BF16)
    roff = jnp.zeros((B, H * T, 1), _F32)
    bn = jnp.tile(sel[:, :, P:P + NEW_ROWS_PAD], (1, H, 1))
    out = _paged_attention(qbd, roff, sel[:, :, :P], cache_k, cache_v, page_table, layer,
                           _pad_rows(k.reshape(B, T, W), NEW_ROWS_PAD),
                           _pad_rows(v.reshape(B, T, W), NEW_ROWS_PAD), bn)
    out = out.reshape(B, H, T, H, dh)
    return jnp.stack([out[:, h, :, h] for h in range(H)], axis=2).reshape(B, T, W)


def _rwkv_mix(xn, x_prev, s0, v_first, vres, mu, w_rkvg, w_o, w_d0, w_d1, w_d2, w_a0, w_a1, w_a2,
              k_k, k_a, r_k, ln_w, ln_b):
    B, T, D = xn.shape
    H = D // HEAD_DIM
    dx = jnp.concatenate([x_prev[:, None, :], xn[:, :-1]], axis=1) - xn
    xr, xw, xk, xv, xa, xg = [xn + dx * mu[j] for j in range(6)]
    r = _mm3(xr, w_rkvg[0])
    k = _mm3(xk, w_rkvg[1])
    v = _mm3(xv, w_rkvg[2])
    gate = jax.nn.silu(_mm3(xg, w_rkvg[3]))
    w_log = -jax.nn.softplus(-(w_d0 + _mm3(jnp.tanh(_mm3(xw, w_d1)), w_d2))) - 0.5
    log_decay = -jnp.exp(w_log)
    a = jax.nn.sigmoid(w_a0 + _mm3(_mm3(xa, w_a1), w_a2))
    if vres is None:
        v_first = v
    else:
        v0, v1, v2 = vres
        v = v + (v_first - v) * jax.nn.sigmoid(v0 + _mm3(_mm3(xv, v1), v2))

    def heads(t):
        return t.reshape(B, T, H, HEAD_DIM)

    r_h, k_h, v_h, a_h, w_h = heads(r), heads(k), heads(v), heads(a), heads(log_decay)
    kk = heads(k * k_k)
    kk = kk * lax.rsqrt(jnp.maximum(jnp.sum(kk * kk, axis=-1, keepdims=True), 1e-24))
    k_h = k_h * (1.0 + (a_h - 1.0) * k_a.reshape(H, HEAD_DIM))
    s_new, y = _wkv7(s0, r_h, w_h, k_h, v_h, kk, a_h)
    mean = jnp.mean(y, axis=-1, keepdims=True)
    var = jnp.mean(jnp.square(y - mean), axis=-1, keepdims=True)
    y = ((y - mean) * lax.rsqrt(var + GN_EPS) * ln_w.reshape(H, HEAD_DIM) + ln_b.reshape(H, HEAD_DIM))
    y = y + jnp.sum(r_h * k_h * r_k, axis=-1, keepdims=True) * v_h
    out = _mm3(y.reshape(B, T, D) * gate, w_o)
    return out, s_new, xn[:, -1], v_first


def _run_group(x, ple, ctx, prm):
    B, T, D = x.shape
    depth = prm['g_mix'].shape[0]
    past = 0 if ctx is None else ctx['page_table'].shape[1] * PAGE_SIZE
    pos = past + jnp.arange(T, dtype=jnp.int32)
    h = x
    v_first = None
    even_rows, odd_rows = [], []
    for i in range(depth):
        hn = _rmsnorm(h, prm['g_mix'][i])
        if i % 2 == 0:
            e = i // 2
            fq, fk, fv, logf, fg, dq, dk, dv, dg, iq, ik, iw = _even_project(
                hn, prm['w_in'][e], prm['b_forget'][e], pos)
            if ctx is None:
                o_f = _fox_prompt(fq, fk, fv, logf)
                o_d = _dsa_prompt(dq, dk, dv, iq, ik, iw)
            else:
                pt = ctx['page_table']
                o_f = _fox_sample(fq, fk, fv, logf, ctx['cache_k_fox'], ctx['cache_v_fox'],
                                  ctx['cache_logf_fox'], pt, e)
                o_d = _dsa_sample(dq, dk, dv, iq, ik, iw, ctx['cache_k_dsa'], ctx['cache_v_dsa'],
                                  ctx['cache_kidx_dsa'], pt, e)
            mixed = jnp.concatenate([o_f * fg, o_d * dg], axis=-1)
            h = h + _mm3(mixed, prm['w_out'][e])
            even_rows.append((fk, fv, logf, dk, dv, ik))
        else:
            o = i // 2
            if ctx is None:
                s0 = jnp.zeros((B, D // HEAD_DIM, HEAD_DIM, HEAD_DIM), x.dtype)
                x_prev = jnp.zeros((B, D), x.dtype)
            else:
                s0, x_prev = ctx['state_wkv'][o], ctx['state_shift'][o]
            vres = None if o == 0 else (prm['w_v0'][o - 1], prm['w_v1'][o - 1], prm['w_v2'][o - 1])
            out, s_new, shift_new, v_first = _rwkv_mix(
                hn, x_prev, s0, v_first, vres, prm['mu_rwkv'][o], prm['w_rkvg'][o], prm['w_o_rwkv'][o],
                prm['w_decay0'][o], prm['w_decay1'][o], prm['w_decay2'][o], prm['w_a0'][o], prm['w_a1'][o],
                prm['w_a2'][o], prm['k_k'][o], prm['k_a'][o], prm['r_k'][o], prm['ln_x_w'][o],
                prm['ln_x_b'][o])
            h = h + out
            odd_rows.append((s_new, shift_new))
        gate = jax.nn.sigmoid(_mm3(_rmsnorm(h, prm['g_ple'][i]), prm['w_ple_gate'][i]))
        h = h + gate * _mm3(ple[i], prm['w_ple_proj'][i])
    y = _rmsnorm(h, prm['g_final'])
    new = [jnp.stack([r[j] for r in even_rows], axis=1) for j in range(6)]
    new += [jnp.stack([r[j] for r in odd_rows], axis=0) for j in range(2)]
    return y, new


def kernel(x_prompt, x_sample, cache_k_fox, cache_v_fox, cache_logf_fox, cache_k_dsa, cache_v_dsa, cache_kidx_dsa, state_wkv, state_shift, page_table, p_prompt, p_sample, g_mix, w_in, b_forget, w_out, mu_rwkv, w_rkvg, w_o_rwkv, w_decay0, w_decay1, w_decay2, w_a0, w_a1, w_a2, w_v0, w_v1, w_v2, k_k, k_a, r_k, ln_x_w, ln_x_b, g_ple, w_ple_gate, w_ple_proj, g_final):
    prm = {'g_mix': g_mix, 'w_in': w_in, 'b_forget': b_forget, 'w_out': w_out, 'mu_rwkv': mu_rwkv,
           'w_rkvg': w_rkvg, 'w_o_rwkv': w_o_rwkv, 'w_decay0': w_decay0, 'w_decay1': w_decay1,
           'w_decay2': w_decay2, 'w_a0': w_a0, 'w_a1': w_a1, 'w_a2': w_a2, 'w_v0': w_v0, 'w_v1': w_v1,
           'w_v2': w_v2, 'k_k': k_k, 'k_a': k_a, 'r_k': r_k, 'ln_x_w': ln_x_w, 'ln_x_b': ln_x_b,
           'g_ple': g_ple, 'w_ple_gate': w_ple_gate, 'w_ple_proj': w_ple_proj, 'g_final': g_final}
    y_prompt, new_p = _run_group(x_prompt, p_prompt, None, prm)
    ctx = {'page_table': page_table, 'cache_k_fox': cache_k_fox, 'cache_v_fox': cache_v_fox,
           'cache_logf_fox': cache_logf_fox, 'cache_k_dsa': cache_k_dsa, 'cache_v_dsa': cache_v_dsa,
           'cache_kidx_dsa': cache_kidx_dsa, 'state_wkv': state_wkv, 'state_shift': state_shift}
    y_sample, new_s = _run_group(x_sample, p_sample, ctx, prm)
    return (y_prompt, y_sample) + tuple(new_p) + tuple(new_s)
```

```python
import functools

import jax
import jax.numpy as jnp
import numpy as np
from jax import lax
from jax.experimental import pallas as pl
from jax.experimental.pallas import tpu as pltpu

HEAD_DIM = 64
H_FOX = 8
H_DSA = 8
H_IDX = 8
D_IDX = 64
W_FOX = H_FOX * HEAD_DIM
W_DSA = H_DSA * HEAD_DIM
TOPK_MAX = 256
ROPE_THETA = 500000.0
ROPE_DIM = HEAD_DIM // 4
PAGE_SIZE = 128
RMS_EPS = 1e-6
GN_EPS = 64e-5
EVEN_SIZES = (W_FOX, W_FOX, W_FOX, H_FOX, W_FOX, W_DSA, W_DSA, W_DSA, W_DSA, H_IDX * D_IDX, D_IDX, H_IDX)
EVEN_CUTS = tuple(int(c) for c in np.cumsum(EVEN_SIZES)[:-1])

V7X_LANES = 128
V7X_VMEM_LIMIT_BYTES = 56 * 1024 * 1024
NEG_BIG = -1e30
INT_MIN = -2 ** 31

_BF16 = jnp.bfloat16
_F32 = jnp.float32


def _dot(a, b):
    return jnp.dot(a, b, preferred_element_type=_F32)


def _dot_nt(a, b):
    return lax.dot_general(a, b, (((1,), (1,)), ((), ())), preferred_element_type=_F32)


def _dot_tn(a, b):
    return lax.dot_general(a, b, (((0,), (0,)), ((), ())), preferred_element_type=_F32)


def _mm_kernel(x_ref, w_ref, o_ref):
    o_ref[...] = _dot(x_ref[...].astype(_BF16), w_ref[...])


def _even_proj_kernel(x_ref, w_ref, ws_ref, *o_refs, seg):
    xb = x_ref[...].astype(_BF16)
    for j, o_ref in enumerate(o_refs[:-1]):
        o_ref[...] = _dot(xb, w_ref[:, j * seg:(j + 1) * seg])
    o_refs[-1][...] = _dot(xb, ws_ref[...])


def _even_proj(hn, w_in):
    B, T, K = hn.shape
    M = B * T
    cols = jnp.split(w_in.astype(_BF16), EVEN_CUTS, axis=-1)
    wide = [0, 1, 2, 4, 5, 6, 7, 8, 9]
    narrow = [3, 10, 11]
    seg = W_FOX
    w_main = jnp.concatenate([cols[i] for i in wide], axis=-1)
    w_small = jnp.concatenate([cols[i] for i in narrow], axis=-1)
    n_small = w_small.shape[1]
    w_small = jnp.pad(w_small, ((0, 0), (0, V7X_LANES - n_small)))
    tm = min(512, M)
    outs = pl.pallas_call(
        functools.partial(_even_proj_kernel, seg=seg),
        grid=(M // tm,),
        in_specs=[pl.BlockSpec((tm, K), lambda i: (i, 0)),
                  pl.BlockSpec((K, seg * len(wide)), lambda i: (0, 0)),
                  pl.BlockSpec((K, V7X_LANES), lambda i: (0, 0))],
        out_specs=[pl.BlockSpec((tm, seg), lambda i: (i, 0))] * len(wide)
        + [pl.BlockSpec((tm, V7X_LANES), lambda i: (i, 0))],
        out_shape=[jax.ShapeDtypeStruct((M, seg), _F32)] * len(wide)
        + [jax.ShapeDtypeStruct((M, V7X_LANES), _F32)],
        compiler_params=pltpu.CompilerParams(
            dimension_semantics=("parallel",),
            vmem_limit_bytes=V7X_VMEM_LIMIT_BYTES),
        name="even_in_proj",
    )(hn.reshape(M, K), w_main, w_small)
    groups = [None] * len(EVEN_SIZES)
    for i, o in zip(wide, outs[:-1]):
        groups[i] = o.reshape(B, T, seg)
    off = 0
    for i in narrow:
        groups[i] = outs[-1][:, off:off + EVEN_SIZES[i]].reshape(B, T, EVEN_SIZES[i])
        off += EVEN_SIZES[i]
    return groups


def _mm(x, w):
    M, K = x.shape
    N = w.shape[1]
    n_pad = (-N) % V7X_LANES
    wb = w.astype(_BF16)
    if n_pad:
        wb = jnp.pad(wb, ((0, 0), (0, n_pad)))
    Np = N + n_pad
    tm = min(512, M)
    tn = Np
    for cand in (1024, 768, 512, 384, 256, 128):
        if Np % cand == 0:
            tn = cand
            break
    assert M % tm == 0
    out = pl.pallas_call(
        _mm_kernel,
        grid=(M // tm, Np // tn),
        in_specs=[pl.BlockSpec((tm, K), lambda i, j: (i, 0)),
                  pl.BlockSpec((K, tn), lambda i, j: (0, j))],
        out_specs=pl.BlockSpec((tm, tn), lambda i, j: (i, j)),
        out_shape=jax.ShapeDtypeStruct((M, Np), _F32),
        compiler_params=pltpu.CompilerParams(
            dimension_semantics=("parallel", "parallel"),
            vmem_limit_bytes=V7X_VMEM_LIMIT_BYTES),
        name="proj_matmul",
    )(x, wb)
    return out[:, :N] if n_pad else out


def _mm3(x, w):
    B, T, K = x.shape
    return _mm(x.reshape(B * T, K), w).reshape(B, T, w.shape[1])


def _causal_pairs(S, tq, tk):
    qi, ki = [], []
    for q in range(S // tq):
        for k in range((q * tq + tq - 1) // tk + 1):
            qi.append(q)
            ki.append(k)
    return jnp.asarray(qi, jnp.int32), jnp.asarray(ki, jnp.int32)


LOG2E = 1.4426950408889634


def _online_softmax_step(s, v, m_sc, l_sc, acc_sc, h):
    m_old = m_sc[h]
    m_new = jnp.maximum(m_old, jnp.max(s, axis=-1, keepdims=True))
    alpha = jnp.exp2(m_old - m_new)
    p = jnp.exp2(s - m_new)
    l_sc[h] = alpha * l_sc[h] + jnp.sum(p, axis=-1, keepdims=True)
    acc_sc[h] = alpha * acc_sc[h] + _dot(p.astype(_BF16), v)
    m_sc[h] = m_new


def _attend_heads(q_ref, k_ref, v_ref, m_sc, l_sc, acc_sc, nh, keep, bias=None):
    s_next = _dot_nt(q_ref[0, 0], k_ref[0, 0])
    for h in range(nh):
        s = s_next
        if h + 1 < nh:
            s_next = _dot_nt(q_ref[0, h + 1], k_ref[0, h + 1])
        if bias is not None:
            s = s - bias[0, h]
        if keep is not None:
            s = jnp.where(keep, s, NEG_BIG)
        _online_softmax_step(s, v_ref[0, h], m_sc, l_sc, acc_sc, h)


def _attn_init(ki, m_sc, l_sc, acc_sc):
    @pl.when(ki == 0)
    def _():
        m_sc[...] = jnp.full(m_sc.shape, NEG_BIG, _F32)
        l_sc[...] = jnp.zeros(l_sc.shape, _F32)
        acc_sc[...] = jnp.zeros(acc_sc.shape, _F32)


def _attn_finish(ki, k_last, o_ref, l_sc, acc_sc, nh):
    @pl.when(ki == k_last)
    def _():
        dh = acc_sc.shape[-1]
        for h in range(nh):
            o_ref[0, :, h * dh:(h + 1) * dh] = acc_sc[h] / l_sc[h]


def _fox_attn_kernel(qi_tab, ki_tab, q_ref, k_ref, v_ref, ck_ref, o_ref, m_sc, l_sc, acc_sc,
                     *, tq, tk, nh):
    p = pl.program_id(1)
    qi = qi_tab[p]
    ki = ki_tab[p]
    k_last = (qi * tq + tq - 1) // tk
    _attn_init(ki, m_sc, l_sc, acc_sc)
    crosses_diagonal = ki * tk + tk - 1 > qi * tq

    @pl.when(crosses_diagonal)
    def _():
        rows = qi * tq + lax.broadcasted_iota(jnp.int32, (tq, tk), 0)
        cols = ki * tk + lax.broadcasted_iota(jnp.int32, (tq, tk), 1)
        _attend_heads(q_ref, k_ref, v_ref, m_sc, l_sc, acc_sc, nh, cols <= rows, ck_ref)

    @pl.when(jnp.logical_not(crosses_diagonal))
    def _():
        _attend_heads(q_ref, k_ref, v_ref, m_sc, l_sc, acc_sc, nh, None, ck_ref)

    _attn_finish(ki, k_last, o_ref, l_sc, acc_sc, nh)


def _fox_prompt(q, k, v, logf, tq=512, tk=1024):
    B, S, H, dh = q.shape
    tq, tk = min(tq, S), min(tk, S)
    ck = jnp.swapaxes(jnp.cumsum(logf, axis=1) * LOG2E, 1, 2)[:, :, None, :]
    W = H * dh
    qh = jnp.swapaxes(q * (HEAD_DIM ** -0.5 * LOG2E), 1, 2).astype(_BF16)
    kh = jnp.swapaxes(k, 1, 2).astype(_BF16)
    vh = jnp.swapaxes(v, 1, 2).astype(_BF16)
    qi_tab, ki_tab = _causal_pairs(S, tq, tk)
    grid_spec = pltpu.PrefetchScalarGridSpec(
        num_scalar_prefetch=2,
        grid=(B, int(qi_tab.shape[0])),
        in_specs=[
            pl.BlockSpec((1, H, tq, dh), lambda b, p, qt, kt: (b, 0, qt[p], 0)),
            pl.BlockSpec((1, H, tk, dh), lambda b, p, qt, kt: (b, 0, kt[p], 0)),
            pl.BlockSpec((1, H, tk, dh), lambda b, p, qt, kt: (b, 0, kt[p], 0)),
            pl.BlockSpec((1, H, 1, tk), lambda b, p, qt, kt: (b, 0, 0, kt[p])),
        ],
        out_specs=pl.BlockSpec((1, tq, W), lambda b, p, qt, kt: (b, qt[p], 0)),
        scratch_shapes=[pltpu.VMEM((H, tq, 1), _F32), pltpu.VMEM((H, tq, 1), _F32),
                        pltpu.VMEM((H, tq, dh), _F32)],
    )
    out = pl.pallas_call(
        functools.partial(_fox_attn_kernel, tq=tq, tk=tk, nh=H),
        grid_spec=grid_spec,
        out_shape=jax.ShapeDtypeStruct((B, S, W), _F32),
        compiler_params=pltpu.CompilerParams(
            dimension_semantics=("parallel", "arbitrary"),
            vmem_limit_bytes=V7X_VMEM_LIMIT_BYTES),
        name="fox_prompt_attn",
    )(qi_tab, ki_tab, qh, kh, vh, ck)
    return out


def _sortable_key(x):
    b = pltpu.bitcast(x, jnp.int32)
    return b ^ ((b >> 31) & jnp.int32(0x7FFFFFFF))


def _topk_rank_threshold(count, nrows, topk):
    kf = float(topk)

    def bits_left(carry):
        i, _, c_v = carry
        return jnp.logical_and(i < 32, jnp.max(jnp.where(c_v != kf, 1.0, 0.0)) > 0.0)

    def bit_step(carry):
        i, v, c_v = carry
        cand = v + jnp.left_shift(jnp.int32(1), 31 - i)
        cnt = count(lambda kk, cols: kk >= cand)
        ok = cnt >= kf
        return i + 1, jnp.where(ok, cand, v), jnp.where(ok, cnt, c_v)

    _, v, _ = lax.while_loop(bits_left, bit_step,
                             (jnp.int32(0), jnp.full((nrows, 1), INT_MIN, jnp.int32),
                              jnp.full((nrows, 1), -1.0, _F32)))
    n_gt = count(lambda kk, cols: kk > v)
    n_ge = count(lambda kk, cols: kk >= v)
    need = kf - n_gt
    has_thr = v != jnp.int32(INT_MIN)
    ties = jnp.max(jnp.where(has_thr & (n_ge > kf), 1.0, 0.0)) > 0.0

    def tie_search(_):
        def idx_step(i, m):
            cand = m + jnp.left_shift(jnp.int32(1), 30 - i)
            cnt = count(lambda kk, cols: (kk == v) & (cols < cand))
            return jnp.where(cnt < need, cand, m)
        return lax.fori_loop(0, 31, idx_step, jnp.zeros((nrows, 1), jnp.int32))

    m_last = lax.cond(ties, tie_search, lambda _: jnp.full((nrows, 1), 2 ** 31 - 1, jnp.int32), 0)
    return v, m_last


def _topk_keep(kk, cols, v, m_last):
    return ((kk > v) | ((kk == v) & (cols <= m_last))) & (kk != jnp.int32(INT_MIN))


def _dsa_select_kernel(iq_ref, iw_ref, ikt_ref, mask_ref, key_sc, *, tq, tk, nk, nh, topk):
    qi = pl.program_id(1)
    nkc = (qi * tq + tq - 1) // tk + 1
    rows = qi * tq + lax.broadcasted_iota(jnp.int32, (tq, tk), 0)
    col0 = lax.broadcasted_iota(jnp.int32, (tq, tk), 1)
    iw = iw_ref[0]

    def score_chunk(c, carry):
        kt = ikt_ref[0, c]
        acc = jnp.zeros((tq, tk), _F32)
        for h in range(nh):
            d = jnp.dot(iq_ref[0, h], kt, preferred_element_type=_F32)
            acc = acc + jnp.maximum(d, 0.0) * iw[:, h:h + 1]
        key = _sortable_key(acc)
        key_sc[c] = jnp.where(c * tk + col0 <= rows, key, jnp.int32(INT_MIN))
        return carry

    lax.fori_loop(0, nkc, score_chunk, 0)

    def count(pred):
        def body(c, part):
            hit = jnp.where(pred(key_sc[c], c * tk + col0), 1.0, 0.0)
            for j in range(tk // V7X_LANES):
                part = part + hit[:, j * V7X_LANES:(j + 1) * V7X_LANES]
            return part
        part = lax.fori_loop(0, nkc, body, jnp.zeros((tq, V7X_LANES), _F32))
        return jnp.sum(part, axis=-1, keepdims=True)

    v, m_last = _topk_rank_threshold(count, tq, topk)

    def write_chunk(c, carry):
        sel = _topk_keep(key_sc[c], c * tk + col0, v, m_last)
        mask_ref[0, 0, c] = sel.astype(jnp.int8)
        return carry

    lax.fori_loop(0, nkc, write_chunk, 0)

    def zero_chunk(c, carry):
        mask_ref[0, 0, c] = jnp.zeros((tq, tk), jnp.int8)
        return carry

    lax.fori_loop(nkc, nk, zero_chunk, 0)


def _dsa_select(iq, ik, iw, topk, tq=256, tk=1024):
    B, S, Hi, Di = iq.shape
    tq, tk = min(tq, S), min(tk, S)
    nq, nk = S // tq, S // tk
    iqh = jnp.swapaxes(iq, 1, 2).astype(_BF16)
    ikt = jnp.swapaxes(ik.reshape(B, nk, tk, Di), 2, 3).astype(_BF16)
    iws = (iw * (D_IDX ** -0.5)).astype(_F32)
    return pl.pallas_call(
        functools.partial(_dsa_select_kernel, tq=tq, tk=tk, nk=nk, nh=Hi, topk=topk),
        grid=(B, nq),
        in_specs=[pl.BlockSpec((1, Hi, tq, Di), lambda b, q: (b, 0, q, 0)),
                  pl.BlockSpec((1, tq, Hi), lambda b, q: (b, q, 0)),
                  pl.BlockSpec((1, nk, Di, tk), lambda b, q: (b, 0, 0, 0))],
        out_specs=pl.BlockSpec((1, 1, nk, tq, tk), lambda b, q: (b, q, 0, 0, 0)),
        out_shape=jax.ShapeDtypeStruct((B, nq, nk, tq, tk), jnp.int8),
        scratch_shapes=[pltpu.VMEM((nk, tq, tk), jnp.int32)],
        compiler_params=pltpu.CompilerParams(
            dimension_semantics=("parallel", "arbitrary"),
            vmem_limit_bytes=V7X_VMEM_LIMIT_BYTES),
        name="dsa_topk_select",
    )(iqh, iws, ikt)


def _dsa_attn_kernel(qi_tab, ki_tab, q_ref, k_ref, v_ref, mask_ref, o_ref, m_sc, l_sc, acc_sc,
                     *, tq, tk, nh):
    p = pl.program_id(1)
    qi = qi_tab[p]
    ki = ki_tab[p]
    k_last = (qi * tq + tq - 1) // tk
    _attn_init(ki, m_sc, l_sc, acc_sc)
    keep = mask_ref[0, :, 0].reshape(tq, tk) != 0
    _attend_heads(q_ref, k_ref, v_ref, m_sc, l_sc, acc_sc, nh, keep)
    _attn_finish(ki, k_last, o_ref, l_sc, acc_sc, nh)


def _dsa_prompt(q, k, v, iq, ik, iw, tq=512, tk=1024, tq_sel=128):
    B, S, H, dh = q.shape
    topk = min(TOPK_MAX, S // 4)
    tq, tk, tq_sel = min(tq, S), min(tk, S), min(tq_sel, S)
    mask = _dsa_select(iq, ik, iw, topk, tq=tq_sel, tk=tk)
    r = tq // tq_sel
    W = H * dh
    qh = jnp.swapaxes(q * (HEAD_DIM ** -0.5 * LOG2E), 1, 2).astype(_BF16)
    kh = jnp.swapaxes(k, 1, 2).astype(_BF16)
    vh = jnp.swapaxes(v, 1, 2).astype(_BF16)
    qi_tab, ki_tab = _causal_pairs(S, tq, tk)
    grid_spec = pltpu.PrefetchScalarGridSpec(
        num_scalar_prefetch=2,
        grid=(B, int(qi_tab.shape[0])),
        in_specs=[
            pl.BlockSpec((1, H, tq, dh), lambda b, p, qt, kt: (b, 0, qt[p], 0)),
            pl.BlockSpec((1, H, tk, dh), lambda b, p, qt, kt: (b, 0, kt[p], 0)),
            pl.BlockSpec((1, H, tk, dh), lambda b, p, qt, kt: (b, 0, kt[p], 0)),
            pl.BlockSpec((1, r, 1, tq_sel, tk), lambda b, p, qt, kt: (b, qt[p], kt[p], 0, 0)),
        ],
        out_specs=pl.BlockSpec((1, tq, W), lambda b, p, qt, kt: (b, qt[p], 0)),
        scratch_shapes=[pltpu.VMEM((H, tq, 1), _F32), pltpu.VMEM((H, tq, 1), _F32),
                        pltpu.VMEM((H, tq, dh), _F32)],
    )
    out = pl.pallas_call(
        functools.partial(_dsa_attn_kernel, tq=tq, tk=tk, nh=H),
        grid_spec=grid_spec,
        out_shape=jax.ShapeDtypeStruct((B, S, W), _F32),
        compiler_params=pltpu.CompilerParams(
            dimension_semantics=("parallel", "arbitrary"),
            vmem_limit_bytes=V7X_VMEM_LIMIT_BYTES),
        name="dsa_prompt_attn",
    )(qi_tab, ki_tab, qh, kh, vh, mask)
    return out


def _split2(x):
    hi = x.astype(_BF16)
    lo = (x - hi.astype(_F32)).astype(_BF16)
    return hi, lo


def _dot3(a, b):
    ah, al = a
    bh, bl = b
    return _dot(ah, bh) + _dot(ah, bl) + _dot(al, bh)


def _wkv_chunk_kernel(r_ref, lw_ref, k_ref, v_ref, al_ref, be_ref, s0_ref, y_ref, sT_ref, s_sc,
                      *, L, n, npair, nchunk):
    t_blk = pl.program_id(2)
    L2, n2 = 2 * L, 2 * n

    @pl.when(t_blk == 0)
    def _():
        s_sc[...] = s0_ref[0]

    ri = lax.broadcasted_iota(jnp.int32, (L2, L2), 0)
    ci = lax.broadcasted_iota(jnp.int32, (L2, L2), 1)
    same_head = (ri // L) == (ci // L)
    strict = same_head & (ci < ri)
    incl = same_head & (ci <= ri)
    bs = min(16, L)
    diag_blk = (ri // bs) == (ci // bs)
    eye = (ri == ci).astype(_F32)
    cum_tri = (lax.broadcasted_iota(jnp.int32, (L, L), 0)
               >= lax.broadcasted_iota(jnp.int32, (L, L), 1)).astype(_BF16)
    sr = lax.broadcasted_iota(jnp.int32, (n2, n2), 0)
    sc = lax.broadcasted_iota(jnp.int32, (n2, n2), 1)
    state_blk = (sr // n) == (sc // n)
    lane_a = lax.broadcasted_iota(jnp.int32, (L, n2), 1) < n

    def expand(x):
        return jnp.concatenate([jnp.where(lane_a, x, 0.0), jnp.where(lane_a, 0.0, x)], axis=0)

    def compact(x):
        return x[:L] + x[L:]

    def each(fn, *cols):
        return [fn(*args) for args in zip(*cols)]

    def chunk(c, carry):
        rows = pl.ds(pl.multiple_of(c * L, L), L)
        lane_sl = [slice(p * n2, (p + 1) * n2) for p in range(npair)]
        lw = [lw_ref[0, rows, ls] for ls in lane_sl]
        r = [r_ref[0, rows, ls] for ls in lane_sl]
        k = [k_ref[0, rows, ls] for ls in lane_sl]
        v = [v_ref[0, rows, ls] for ls in lane_sl]
        al = [al_ref[0, rows, ls] for ls in lane_sl]
        be = [be_ref[0, rows, ls] for ls in lane_sl]

        def cumsum(x):
            hi = x.astype(_BF16)
            r1 = x - hi.astype(_F32)
            mid = r1.astype(_BF16)
            lo = (r1 - mid.astype(_F32)).astype(_BF16)
            return _dot(cum_tri, hi) + _dot(cum_tri, mid) + _dot(cum_tri, lo)

        g = each(cumsum, lw)
        g_last = each(lambda x: x[L - 1:L, :], g)
        a_t = each(lambda x, gg, l: x * jnp.exp(gg - l), al, g, lw)
        r_t = each(lambda x, gg: x * jnp.exp(gg), r, g)
        eng = each(lambda gg: jnp.exp(-gg), g)
        pa_pr = each(lambda x, y: jnp.concatenate([expand(x), expand(y)], axis=0).astype(_BF16), a_t, r_t)
        pb = each(lambda x, e: expand(x * e).astype(_BF16), be, eng)
        pk = each(lambda x, e: expand(x * e).astype(_BF16), k, eng)
        pv = each(lambda x: expand(x).astype(_BF16), v)
        xb = each(_dot_nt, pa_pr, pb)
        xk = each(_dot_nt, pa_pr, pk)
        m_ab = each(lambda x: jnp.where(strict, x[:L2], 0.0), xb)
        m_ak = each(lambda x: jnp.where(strict, x[:L2], 0.0).astype(_BF16), xk)
        n_rb = each(lambda x: jnp.where(incl, x[L2:], 0.0).astype(_BF16), xb)
        n_rk = each(lambda x: jnp.where(incl, x[L2:], 0.0).astype(_BF16), xk)
        d1 = each(lambda x: jnp.where(diag_blk, x, 0.0), m_ab)
        e_b = each(lambda x, d: (x - d).astype(_BF16), m_ab, d1)
        d1s = each(_split2, d1)
        d2 = each(_dot3, d1s, d1s)
        d2s = each(_split2, d2)
        d4 = each(_dot3, d2s, d2s)
        d4s = each(_split2, d4)
        d8 = each(_dot3, d4s, d4s)
        p12 = each(lambda x, y: _dot3(_split2(eye + x), _split2(eye + y)), d1, d2)
        p48 = each(lambda x, y: _dot3(_split2(eye + x), _split2(eye + y)), d4, d8)
        x_b = each(lambda x, y: _dot3(_split2(x), _split2(y)).astype(_BF16), p12, p48)
        nn = each(_dot, x_b, e_b)
        nn_b = each(lambda x: x.astype(_BF16), nn)
        nn2 = each(_dot, nn_b, nn_b)
        q12 = each(lambda x, y: _dot((eye + x).astype(_BF16), (eye + y).astype(_BF16)), nn, nn2)
        tinv = each(lambda x, y: _dot(x.astype(_BF16), y).astype(_BF16), q12, x_b)
        s = [s_sc[p] for p in range(npair)]
        ars = each(lambda x, y, st: _dot_nt(jnp.concatenate([x, y], axis=0).astype(_BF16), st.astype(_BF16)),
                   a_t, r_t, s)
        rhs = each(lambda x, m, vv: (expand(x[:L]) + _dot(m, vv)).astype(_BF16), ars, m_ak, pv)
        u_e = each(_dot, tinv, rhs)
        y_e = each(lambda x, nb, u, nk, vv: expand(x[L:]) + _dot(nb, u.astype(_BF16)) + _dot(nk, vv),
                   ars, n_rb, u_e, n_rk, pv)
        for ls, ye in zip(lane_sl, y_e):
            y_ref[0, rows, ls] = compact(ye)
        dec = each(lambda gl, gg: jnp.exp(gl - gg), g_last, g)
        uv = each(lambda u, vv: jnp.concatenate([compact(u), vv], axis=0).astype(_BF16), u_e, v)
        bk = each(lambda x, y, d: jnp.concatenate([x * d, y * d], axis=0).astype(_BF16), be, k, dec)
        upd = each(_dot_tn, uv, bk)
        for p in range(npair):
            s_sc[p] = s[p] * jnp.exp(g_last[p]) + jnp.where(state_blk, upd[p], 0.0)
        return carry

    lax.fori_loop(0, nchunk, chunk, 0)

    @pl.when(t_blk == pl.num_programs(2) - 1)
    def _():
        sT_ref[0] = s_sc[...]


def _wkv7(s0, r, lw, k, v, kk, a, L=64, tc=256, npair=8):
    B, T, H, N = r.shape
    L, tc = min(L, T), min(tc, T)
    npair = min(npair, H // 2)
    D = H * N
    flat = lambda x: x.reshape(B, T, D)
    s0p = s0.reshape(B, H // 2, 2, N, N)
    z = jnp.zeros_like(s0p[:, :, 0])
    s0bd = jnp.concatenate([jnp.concatenate([s0p[:, :, 0], z], axis=-1),
                            jnp.concatenate([z, s0p[:, :, 1]], axis=-1)], axis=-2)
    seq_spec = pl.BlockSpec((1, tc, npair * 2 * N), lambda b, g, t: (b, t, g))
    st_spec = pl.BlockSpec((1, npair, 2 * N, 2 * N), lambda b, g, t: (b, g, 0, 0))
    y, sT = pl.pallas_call(
        functools.partial(_wkv_chunk_kernel, L=L, n=N, npair=npair, nchunk=tc // L),
        grid=(B, H // (2 * npair), T // tc),
        in_specs=[seq_spec] * 6 + [st_spec],
        out_specs=[seq_spec, st_spec],
        out_shape=[jax.ShapeDtypeStruct((B, T, D), _F32),
                   jax.ShapeDtypeStruct((B, H // 2, 2 * N, 2 * N), _F32)],
        scratch_shapes=[pltpu.VMEM((npair, 2 * N, 2 * N), _F32)],
        compiler_params=pltpu.CompilerParams(
            dimension_semantics=("parallel", "parallel", "arbitrary"),
            vmem_limit_bytes=V7X_VMEM_LIMIT_BYTES),
        name="wkv7_chunked",
    )(flat(r), flat(lw), flat(k), flat(v), flat(-kk), flat(kk * a), s0bd)
    sT = jnp.stack([sT[:, :, :N, :N], sT[:, :, N:, N:]], axis=2).reshape(B, H, N, N)
    return sT, y.reshape(B, T, H, N)


def _rmsnorm(x, g):
    y = x * lax.rsqrt(jnp.mean(x * x, axis=-1, keepdims=True) + RMS_EPS)
    return y * g


def _partial_rope(x, pos):
    half = ROPE_DIM // 2
    inv = ROPE_THETA ** (-2.0 * jnp.arange(half, dtype=_F32) / ROPE_DIM)
    ang = pos.astype(_F32)[:, None] * inv
    cos, sin = jnp.cos(ang)[:, None, :], jnp.sin(ang)[:, None, :]
    x1, x2 = x[..., :half], x[..., half:ROPE_DIM]
    return jnp.concatenate([x1 * cos - x2 * sin, x2 * cos + x1 * sin, x[..., ROPE_DIM:]], axis=-1)


def _even_project(hn, w_in, b_forget, pos):
    B, T, _ = hn.shape
    fq, fk, fv, ff, fg, dq, dk, dv, dg, iq, ik, iw = _even_proj(hn, w_in)

    def heads(t, h):
        return t.reshape(B, T, h, -1)

    logf = jax.nn.log_sigmoid(ff + b_forget)
    dq = _partial_rope(heads(dq, H_DSA), pos)
    dk = _partial_rope(heads(dk, H_DSA), pos)
    iq = _partial_rope(heads(iq, H_IDX), pos)
    ik = _partial_rope(ik[:, :, None, :], pos)[:, :, 0]
    iw = iw * (H_IDX ** -0.5)
    return (heads(fq, H_FOX), heads(fk, H_FOX), heads(fv, H_FOX), logf, jax.nn.silu(fg),
            dq, dk, heads(dv, H_DSA), jax.nn.silu(dg), iq, ik, iw)


PAGES_PER_STEP = 8
NEW_ROWS_PAD = 16


def _paged_attn_kernel(pt_ref, q_ref, roff_ref, bias_ref, *refs, pg, page):
    k_refs, v_refs = refs[:pg], refs[pg:2 * pg]
    kn_ref, vn_ref, bn_ref, o_ref, m_sc, l_sc, acc_sc = refs[2 * pg:]
    j = pl.program_id(1)

    @pl.when(j == 0)
    def _():
        m_sc[...] = jnp.full(m_sc.shape, NEG_BIG, _F32)
        l_sc[...] = jnp.zeros(l_sc.shape, _F32)
        acc_sc[...] = jnp.zeros(acc_sc.shape, _F32)

    q = q_ref[0]
    roff = roff_ref[0]
    nrep = q.shape[0] // bias_ref.shape[1]

    def update(scores, values, weigh):
        m_old = m_sc[...]
        m_new = m_old
        for s in scores:
            m_new = jnp.maximum(m_new, jnp.max(s, axis=-1, keepdims=True))
        alpha = jnp.exp(m_old - m_new)
        l_new = alpha * l_sc[...]
        acc = alpha * acc_sc[...]
        for s, v in zip(scores, values):
            p = jnp.exp(s - m_new)
            l_new = l_new + jnp.sum(p, axis=-1, keepdims=True)
            acc = acc + weigh(p.astype(_BF16), v)
        l_sc[...] = l_new
        acc_sc[...] = acc
        m_sc[...] = m_new

    def page_t(ref):
        return ref[0, 0].reshape(q.shape[1], page).astype(_BF16)

    scores = []
    for i in range(pg):
        b = bias_ref[0, :, i * page:(i + 1) * page]
        scores.append(_dot(q, page_t(k_refs[i])) + roff + jnp.concatenate([b] * nrep, axis=0))
    update(scores, [page_t(v_refs[i]) for i in range(pg)], _dot_nt)

    @pl.when(j == pl.num_programs(1) - 1)
    def _():
        update([_dot_nt(q, kn_ref[0].astype(_BF16)) + bn_ref[0]], [vn_ref[0].astype(_BF16)], _dot)
        o_ref[0] = acc_sc[...] / l_sc[...]


def _paged_attention(qbd, roff, bias_t, cache_k, cache_v, page_table, layer, k_new, v_new, bias_new):
    B, R, W = qbd.shape
    nh, dh = cache_k.shape[3:]
    cache_k = jnp.transpose(cache_k, (0, 1, 3, 4, 2))
    cache_v = jnp.transpose(cache_v, (0, 1, 3, 4, 2))
    npages = page_table.shape[1]
    pg = min(PAGES_PER_STEP, npages)
    assert npages % pg == 0
    tn = k_new.shape[1]
    tile_rows = bias_t.shape[1]

    def page_spec(i):
        return pl.BlockSpec((1, 1, nh, dh, PAGE_SIZE),
                            lambda b, j, pt, i=i: (pt[b * npages + j * pg + i], layer, 0, 0, 0))

    grid_spec = pltpu.PrefetchScalarGridSpec(
        num_scalar_prefetch=1,
        grid=(B, npages // pg),
        in_specs=[pl.BlockSpec((1, R, W), lambda b, j, pt: (b, 0, 0)),
                  pl.BlockSpec((1, R, 1), lambda b, j, pt: (b, 0, 0)),
                  pl.BlockSpec((1, tile_rows, pg * PAGE_SIZE), lambda b, j, pt: (b, 0, j))]
        + [page_spec(i) for i in range(pg)] + [page_spec(i) for i in range(pg)]
        + [pl.BlockSpec((1, tn, W), lambda b, j, pt: (b, 0, 0)),
           pl.BlockSpec((1, tn, W), lambda b, j, pt: (b, 0, 0)),
           pl.BlockSpec((1, R, tn), lambda b, j, pt: (b, 0, 0))],
        out_specs=pl.BlockSpec((1, R, W), lambda b, j, pt: (b, 0, 0)),
        scratch_shapes=[pltpu.VMEM((R, 1), _F32), pltpu.VMEM((R, 1), _F32), pltpu.VMEM((R, W), _F32)],
    )
    return pl.pallas_call(
        functools.partial(_paged_attn_kernel, pg=pg, page=PAGE_SIZE),
        grid_spec=grid_spec,
        out_shape=jax.ShapeDtypeStruct((B, R, W), _F32),
        compiler_params=pltpu.CompilerParams(
            dimension_semantics=("parallel", "arbitrary"),
            vmem_limit_bytes=V7X_VMEM_LIMIT_BYTES),
        name="paged_decode_attn",
    )(page_table.reshape(-1), qbd, roff, bias_t, *([cache_k] * pg), *([cache_v] * pg), k_new, v_new, bias_new)


def _pad_rows(x, rows, value=0.0):
    return jnp.pad(x, ((0, 0), (0, rows - x.shape[1]), (0, 0)), constant_values=value)


def _fox_sample(q, k, v, logf, cache_k, cache_v, cache_logf, page_table, layer):
    B, T, H, dh = q.shape
    W = H * dh
    logf_past = cache_logf[page_table, layer].reshape(B, -1, H)
    c_past = jnp.cumsum(logf_past, axis=1)
    c_new = c_past[:, -1:] + jnp.cumsum(logf, axis=1)
    eye = jnp.eye(H, dtype=_F32)
    qs = q * (HEAD_DIM ** -0.5)
    qbd = (qs[:, :, :, None, :] * eye[None, None, :, :, None]).reshape(B, T * H, W).astype(_BF16)
    roff = c_new.reshape(B, T * H, 1)
    bias_t = -jnp.swapaxes(c_past, 1, 2)
    tri = jnp.arange(T)[None, :] <= jnp.arange(T)[:, None]
    bn = c_new[:, :, None, :] - c_new[:, None, :, :]
    bn = jnp.where(tri[None, :, :, None], bn, NEG_BIG)
    bn = jnp.swapaxes(bn, 2, 3).reshape(B, T * H, T)
    bn = jnp.pad(bn, ((0, 0), (0, 0), (0, NEW_ROWS_PAD - T)), constant_values=NEG_BIG)
    out = _paged_attention(qbd, roff, bias_t, cache_k, cache_v, page_table, layer,
                           _pad_rows(k.reshape(B, T, W), NEW_ROWS_PAD),
                           _pad_rows(v.reshape(B, T, W), NEW_ROWS_PAD), bn)
    out = out.reshape(B, T, H, H, dh)
    return jnp.stack([out[:, :, h, h] for h in range(H)], axis=2).reshape(B, T, W)


def _dsa_sample_select_kernel(pt_ref, x_ref, w_ref, kn_ref, *refs, npages, page, nq, nh, topk):
    kp_refs = refs[:npages]
    bias_ref, key_sc = refs[npages:]
    i = pl.program_id(1)
    x = x_ref[0]
    w = w_ref[0]
    r0 = pl.multiple_of(i * nq, nq)

    def fold(d):
        t = jnp.maximum(d, 0.0) * w
        acc = t[0:nq]
        for h in range(1, nh):
            acc = acc + t[h * nq:(h + 1) * nq]
        return acc

    for p in range(npages):
        d = _dot(x, kp_refs[p][0, 0].astype(_BF16))
        key_sc[pl.ds(r0, nq), p * page:(p + 1) * page] = _sortable_key(fold(d))
    dn = _dot_nt(x, kn_ref[0].astype(_BF16))
    qrow = lax.broadcasted_iota(jnp.int32, (nq, page), 0)
    col = lax.broadcasted_iota(jnp.int32, (nq, page), 1)
    key_sc[pl.ds(r0, nq), npages * page:(npages + 1) * page] = jnp.where(
        col <= qrow, _sortable_key(fold(dn)), jnp.int32(INT_MIN))

    @pl.when(i == pl.num_programs(1) - 1)
    def _():
        nrows, ncols = key_sc.shape
        cols = lax.broadcasted_iota(jnp.int32, (nrows, ncols), 1)

        def count(pred):
            return jnp.sum(jnp.where(pred(key_sc[...], cols), 1.0, 0.0), axis=-1, keepdims=True)

        v, m_last = _topk_rank_threshold(count, nrows, topk)
        bias_ref[0] = jnp.where(_topk_keep(key_sc[...], cols, v, m_last), 0.0, NEG_BIG)


def _dsa_sample_select(iq, ik, iw, cache_kidx, page_table, layer, topk, gb=8):
    B, T, Hi, Di = iq.shape
    npages = page_table.shape[1]
    gb = min(gb, B)
    assert B % gb == 0
    ncols = (npages + 1) * PAGE_SIZE
    x = jnp.swapaxes(iq, 1, 2).reshape(B, Hi * T, Di).astype(_BF16)
    w = jnp.swapaxes(iw * (D_IDX ** -0.5), 1, 2).reshape(B, Hi * T, 1)
    kn = _pad_rows(ik, PAGE_SIZE)

    def page_spec(p):
        return pl.BlockSpec((1, 1, Di, PAGE_SIZE),
                            lambda g, i, pt, p=p: (pt[(g * gb + i) * npages + p], layer, 0, 0))

    grid_spec = pltpu.PrefetchScalarGridSpec(
        num_scalar_prefetch=1,
        grid=(B // gb, gb),
        in_specs=[pl.BlockSpec((1, Hi * T, Di), lambda g, i, pt: (g * gb + i, 0, 0)),
                  pl.BlockSpec((1, Hi * T, 1), lambda g, i, pt: (g * gb + i, 0, 0)),
                  pl.BlockSpec((1, PAGE_SIZE, Di), lambda g, i, pt: (g * gb + i, 0, 0))]
        + [page_spec(p) for p in range(npages)],
        out_specs=pl.BlockSpec((1, gb * T, ncols), lambda g, i, pt: (g, 0, 0)),
        scratch_shapes=[pltpu.VMEM((gb * T, ncols), jnp.int32)],
    )
    bias = pl.pallas_call(
        functools.partial(_dsa_sample_select_kernel, npages=npages, page=PAGE_SIZE, nq=T, nh=Hi, topk=topk),
        grid_spec=grid_spec,
        out_shape=jax.ShapeDtypeStruct((B // gb, gb * T, ncols), _F32),
        compiler_params=pltpu.CompilerParams(
            dimension_semantics=("parallel", "arbitrary"),
            vmem_limit_bytes=V7X_VMEM_LIMIT_BYTES),
        name="dsa_sample_select",
    )(page_table.reshape(-1), x, w, kn, *([jnp.swapaxes(cache_kidx, 2, 3)] * npages))
    return bias.reshape(B, T, ncols)


def _dsa_sample(q, k, v, iq, ik, iw, cache_k, cache_v, cache_kidx, page_table, layer):
    B, T, H, dh = q.shape
    W = H * dh
    P = page_table.shape[1] * PAGE_SIZE
    topk = min(TOPK_MAX, (P + T) // 4)
    sel = _dsa_sample_select(iq, ik, iw, cache_kidx, page_table, layer, topk)
    eye = jnp.eye(H, dtype=_F32)
    qh = jnp.swapaxes(q * (HEAD_DIM ** -0.5), 1, 2)
    qbd = (qh[:, :, :, None, :] * eye[None, :, None, :, None]).reshape(B, H * T, W).astype(_BF16)
    roff = jnp.zeros((B, H * T, 1), _F32)
    bn = jnp.tile(sel[:, :, P:P + NEW_ROWS_PAD], (1, H, 1))
    out = _paged_attention(qbd, roff, sel[:, :, :P], cache_k, cache_v, page_table, layer,
                           _pad_rows(k.reshape(B, T, W), NEW_ROWS_PAD),
                           _pad_rows(v.reshape(B, T, W), NEW_ROWS_PAD), bn)
    out = out.reshape(B, H, T, H, dh)
    return jnp.stack([out[:, h, :, h] for h in range(H)], axis=2).reshape(B, T, W)


def _rwkv_mix(xn, x_prev, s0, v_first, vres, mu, w_rkvg, w_o, w_d0, w_d1, w_d2, w_a0, w_a1, w_a2,
              k_k, k_a, r_k, ln_w, ln_b):
    B, T, D = xn.shape
    H = D // HEAD_DIM
    dx = jnp.concatenate([x_prev[:, None, :], xn[:, :-1]], axis=1) - xn
    xr, xw, xk, xv, xa, xg = [xn + dx * mu[j] for j in range(6)]
    r = _mm3(xr, w_rkvg[0])
    k = _mm3(xk, w_rkvg[1])
    v = _mm3(xv, w_rkvg[2])
    gate = jax.nn.silu(_mm3(xg, w_rkvg[3]))
    w_log = -jax.nn.softplus(-(w_d0 + _mm3(jnp.tanh(_mm3(xw, w_d1)), w_d2))) - 0.5
    log_decay = -jnp.exp(w_log)
    a = jax.nn.sigmoid(w_a0 + _mm3(_mm3(xa, w_a1), w_a2))
    if vres is None:
        v_first = v
    else:
        v0, v1, v2 = vres
        v = v + (v_first - v) * jax.nn.sigmoid(v0 + _mm3(_mm3(xv, v1), v2))

    def heads(t):
        return t.reshape(B, T, H, HEAD_DIM)

    r_h, k_h, v_h, a_h, w_h = heads(r), heads(k), heads(v), heads(a), heads(log_decay)
    kk = heads(k * k_k)
    kk = kk * lax.rsqrt(jnp.maximum(jnp.sum(kk * kk, axis=-1, keepdims=True), 1e-24))
    k_h = k_h * (1.0 + (a_h - 1.0) * k_a.reshape(H, HEAD_DIM))
    s_new, y = _wkv7(s0, r_h, w_h, k_h, v_h, kk, a_h)
    mean = jnp.mean(y, axis=-1, keepdims=True)
    var = jnp.mean(jnp.square(y - mean), axis=-1, keepdims=True)
    y = ((y - mean) * lax.rsqrt(var + GN_EPS) * ln_w.reshape(H, HEAD_DIM) + ln_b.reshape(H, HEAD_DIM))
    y = y + jnp.sum(r_h * k_h * r_k, axis=-1, keepdims=True) * v_h
    out = _mm3(y.reshape(B, T, D) * gate, w_o)
    return out, s_new, xn[:, -1], v_first


def _run_group(x, ple, ctx, prm):
    B, T, D = x.shape
    depth = prm['g_mix'].shape[0]
    past = 0 if ctx is None else ctx['page_table'].shape[1] * PAGE_SIZE
    pos = past + jnp.arange(T, dtype=jnp.int32)
    h = x
    v_first = None
    even_rows, odd_rows = [], []
    for i in range(depth):
        hn = _rmsnorm(h, prm['g_mix'][i])
        if i % 2 == 0:
            e = i // 2
            fq, fk, fv, logf, fg, dq, dk, dv, dg, iq, ik, iw = _even_project(
                hn, prm['w_in'][e], prm['b_forget'][e], pos)
            if ctx is None:
                o_f = _fox_prompt(fq, fk, fv, logf)
                o_d = _dsa_prompt(dq, dk, dv, iq, ik, iw)
            else:
                pt = ctx['page_table']
                o_f = _fox_sample(fq, fk, fv, logf, ctx['cache_k_fox'], ctx['cache_v_fox'],
                                  ctx['cache_logf_fox'], pt, e)
                o_d = _dsa_sample(dq, dk, dv, iq, ik, iw, ctx['cache_k_dsa'], ctx['cache_v_dsa'],
                                  ctx['cache_kidx_dsa'], pt, e)
            mixed = jnp.concatenate([o_f * fg, o_d * dg], axis=-1)
            h = h + _mm3(mixed, prm['w_out'][e])
            even_rows.append((fk, fv, logf, dk, dv, ik))
        else:
            o = i // 2
            if ctx is None:
                s0 = jnp.zeros((B, D // HEAD_DIM, HEAD_DIM, HEAD_DIM), x.dtype)
                x_prev = jnp.zeros((B, D), x.dtype)
            else:
                s0, x_prev = ctx['state_wkv'][o], ctx['state_shift'][o]
            vres = None if o == 0 else (prm['w_v0'][o - 1], prm['w_v1'][o - 1], prm['w_v2'][o - 1])
            out, s_new, shift_new, v_first = _rwkv_mix(
                hn, x_prev, s0, v_first, vres, prm['mu_rwkv'][o], prm['w_rkvg'][o], prm['w_o_rwkv'][o],
                prm['w_decay0'][o], prm['w_decay1'][o], prm['w_decay2'][o], prm['w_a0'][o], prm['w_a1'][o],
                prm['w_a2'][o], prm['k_k'][o], prm['k_a'][o], prm['r_k'][o], prm['ln_x_w'][o],
                prm['ln_x_b'][o])
            h = h + out
            odd_rows.append((s_new, shift_new))
        gate = jax.nn.sigmoid(_mm3(_rmsnorm(h, prm['g_ple'][i]), prm['w_ple_gate'][i]))
        h = h + gate * _mm3(ple[i], prm['w_ple_proj'][i])
    y = _rmsnorm(h, prm['g_final'])
    new = [jnp.stack([r[j] for r in even_rows], axis=1) for j in range(6)]
    new += [jnp.stack([r[j] for r in odd_rows], axis=0) for j in range(2)]
    return y, new


def kernel(x_prompt, x_sample, cache_k_fox, cache_v_fox, cache_logf_fox, cache_k_dsa, cache_v_dsa, cache_kidx_dsa, state_wkv, state_shift, page_table, p_prompt, p_sample, g_mix, w_in, b_forget, w_out, mu_rwkv, w_rkvg, w_o_rwkv, w_decay0, w_decay1, w_decay2, w_a0, w_a1, w_a2, w_v0, w_v1, w_v2, k_k, k_a, r_k, ln_x_w, ln_x_b, g_ple, w_ple_gate, w_ple_proj, g_final):
    prm = {'g_mix': g_mix, 'w_in': w_in, 'b_forget': b_forget, 'w_out': w_out, 'mu_rwkv': mu_rwkv,
           'w_rkvg': w_rkvg, 'w_o_rwkv': w_o_rwkv, 'w_decay0': w_decay0, 'w_decay1': w_decay1,
           'w_decay2': w_decay2, 'w_a0': w_a0, 'w_a1': w_a1, 'w_a2': w_a2, 'w_v0': w_v0, 'w_v1': w_v1,
           'w_v2': w_v2, 'k_k': k_k, 'k_a': k_a, 'r_k': r_k, 'ln_x_w': ln_x_w, 'ln_x_b': ln_x_b,
           'g_ple': g_ple, 'w_ple_gate': w_ple_gate, 'w_ple_proj': w_ple_proj, 'g_final': g_final}
    y_prompt, new_p = _run_group(x_prompt, p_prompt, None, prm)
    ctx = {'page_table': page_table, 'cache_k_fox': cache_k_fox, 'cache_v_fox': cache_v_fox,
           'cache_logf_fox': cache_logf_fox, 'cache_k_dsa': cache_k_dsa, 'cache_v_dsa': cache_v_dsa,
           'cache_kidx_dsa': cache_kidx_dsa, 'state_wkv': state_wkv, 'state_shift': state_shift}
    y_sample, new_s = _run_group(x_sample, p_sample, ctx, prm)
    return (y_prompt, y_sample) + tuple(new_p) + tuple(new_s)
```

```python
import functools

import jax
import jax.numpy as jnp
import numpy as np
from jax import lax
from jax.experimental import pallas as pl
from jax.experimental.pallas import tpu as pltpu

HEAD_DIM = 64
H_FOX = 8
H_DSA = 8
H_IDX = 8
D_IDX = 64
W_FOX = H_FOX * HEAD_DIM
W_DSA = H_DSA * HEAD_DIM
TOPK_MAX = 256
ROPE_THETA = 500000.0
ROPE_DIM = HEAD_DIM // 4
PAGE_SIZE = 128
RMS_EPS = 1e-6
GN_EPS = 64e-5
EVEN_SIZES = (W_FOX, W_FOX, W_FOX, H_FOX, W_FOX, W_DSA, W_DSA, W_DSA, W_DSA, H_IDX * D_IDX, D_IDX, H_IDX)
EVEN_CUTS = tuple(int(c) for c in np.cumsum(EVEN_SIZES)[:-1])

V7X_LANES = 128
V7X_VMEM_LIMIT_BYTES = 56 * 1024 * 1024
NEG_BIG = -1e30
INT_MIN = -2 ** 31

_BF16 = jnp.bfloat16
_F32 = jnp.float32


def _dot(a, b):
    return jnp.dot(a, b, preferred_element_type=_F32)


def _dot_nt(a, b):
    return lax.dot_general(a, b, (((1,), (1,)), ((), ())), preferred_element_type=_F32)


def _dot_tn(a, b):
    return lax.dot_general(a, b, (((0,), (0,)), ((), ())), preferred_element_type=_F32)


def _mm_kernel(x_ref, w_ref, o_ref):
    o_ref[...] = _dot(x_ref[...], w_ref[...])


def _even_proj_kernel(x_ref, w_ref, ws_ref, *o_refs, seg):
    xb = x_ref[...]
    for j, o_ref in enumerate(o_refs[:-1]):
        o_ref[...] = _dot(xb, w_ref[:, j * seg:(j + 1) * seg])
    o_refs[-1][...] = _dot(xb, ws_ref[...])


def _even_proj(hn, w_in):
    B, T, K = hn.shape
    M = B * T
    cols = jnp.split(w_in.astype(_BF16), EVEN_CUTS, axis=-1)
    wide = [0, 1, 2, 4, 5, 6, 7, 8, 9]
    narrow = [3, 10, 11]
    seg = W_FOX
    w_main = jnp.concatenate([cols[i] for i in wide], axis=-1)
    w_small = jnp.concatenate([cols[i] for i in narrow], axis=-1)
    n_small = w_small.shape[1]
    w_small = jnp.pad(w_small, ((0, 0), (0, V7X_LANES - n_small)))
    tm = min(512, M)
    outs = pl.pallas_call(
        functools.partial(_even_proj_kernel, seg=seg),
        grid=(M // tm,),
        in_specs=[pl.BlockSpec((tm, K), lambda i: (i, 0)),
                  pl.BlockSpec((K, seg * len(wide)), lambda i: (0, 0)),
                  pl.BlockSpec((K, V7X_LANES), lambda i: (0, 0))],
        out_specs=[pl.BlockSpec((tm, seg), lambda i: (i, 0))] * len(wide)
        + [pl.BlockSpec((tm, V7X_LANES), lambda i: (i, 0))],
        out_shape=[jax.ShapeDtypeStruct((M, seg), _F32)] * len(wide)
        + [jax.ShapeDtypeStruct((M, V7X_LANES), _F32)],
        compiler_params=pltpu.CompilerParams(
            dimension_semantics=("parallel",),
            vmem_limit_bytes=V7X_VMEM_LIMIT_BYTES),
        name="even_in_proj",
    )(hn.reshape(M, K).astype(_BF16), w_main, w_small)
    groups = [None] * len(EVEN_SIZES)
    for i, o in zip(wide, outs[:-1]):
        groups[i] = o.reshape(B, T, seg)
    off = 0
    for i in narrow:
        groups[i] = outs[-1][:, off:off + EVEN_SIZES[i]].reshape(B, T, EVEN_SIZES[i])
        off += EVEN_SIZES[i]
    return groups


def _mm(x, w):
    M, K = x.shape
    N = w.shape[1]
    n_pad = (-N) % V7X_LANES
    wb = w.astype(_BF16)
    if n_pad:
        wb = jnp.pad(wb, ((0, 0), (0, n_pad)))
    Np = N + n_pad
    tm = min(512, M)
    tn = Np
    for cand in (1024, 768, 512, 384, 256, 128):
        if Np % cand == 0:
            tn = cand
            break
    assert M % tm == 0
    out = pl.pallas_call(
        _mm_kernel,
        grid=(M // tm, Np // tn),
        in_specs=[pl.BlockSpec((tm, K), lambda i, j: (i, 0)),
                  pl.BlockSpec((K, tn), lambda i, j: (0, j))],
        out_specs=pl.BlockSpec((tm, tn), lambda i, j: (i, j)),
        out_shape=jax.ShapeDtypeStruct((M, Np), _F32),
        compiler_params=pltpu.CompilerParams(
            dimension_semantics=("parallel", "parallel"),
            vmem_limit_bytes=V7X_VMEM_LIMIT_BYTES),
        name="proj_matmul",
    )(x.astype(_BF16), wb)
    return out[:, :N] if n_pad else out


def _mm3(x, w):
    B, T, K = x.shape
    return _mm(x.reshape(B * T, K), w).reshape(B, T, w.shape[1])


def _causal_pairs(S, tq, tk):
    qi, ki = [], []
    for q in range(S // tq):
        for k in range((q * tq + tq - 1) // tk + 1):
            qi.append(q)
            ki.append(k)
    return jnp.asarray(qi, jnp.int32), jnp.asarray(ki, jnp.int32)


LOG2E = 1.4426950408889634


def _online_softmax_step(s, v, m_sc, l_sc, acc_sc, h):
    m_old = m_sc[h]
    m_new = jnp.maximum(m_old, jnp.max(s, axis=-1, keepdims=True))
    alpha = jnp.exp2(m_old - m_new)
    p = jnp.exp2(s - m_new)
    l_sc[h] = alpha * l_sc[h] + jnp.sum(p, axis=-1, keepdims=True)
    acc_sc[h] = alpha * acc_sc[h] + _dot(p.astype(_BF16), v)
    m_sc[h] = m_new


def _attend_heads(q_ref, k_ref, v_ref, m_sc, l_sc, acc_sc, nh, keep, bias=None):
    s_next = _dot_nt(q_ref[0, 0], k_ref[0, 0])
    for h in range(nh):
        s = s_next
        if h + 1 < nh:
            s_next = _dot_nt(q_ref[0, h + 1], k_ref[0, h + 1])
        if bias is not None:
            s = s - bias[0, h]
        if keep is not None:
            s = jnp.where(keep, s, NEG_BIG)
        _online_softmax_step(s, v_ref[0, h], m_sc, l_sc, acc_sc, h)


def _attn_init(ki, m_sc, l_sc, acc_sc):
    @pl.when(ki == 0)
    def _():
        m_sc[...] = jnp.full(m_sc.shape, NEG_BIG, _F32)
        l_sc[...] = jnp.zeros(l_sc.shape, _F32)
        acc_sc[...] = jnp.zeros(acc_sc.shape, _F32)


def _attn_finish(ki, k_last, o_ref, l_sc, acc_sc, nh):
    @pl.when(ki == k_last)
    def _():
        dh = acc_sc.shape[-1]
        for h in range(nh):
            o_ref[0, :, h * dh:(h + 1) * dh] = acc_sc[h] / l_sc[h]


def _fox_attn_kernel(qi_tab, ki_tab, q_ref, k_ref, v_ref, ck_ref, o_ref, m_sc, l_sc, acc_sc,
                     *, tq, tk, nh):
    p = pl.program_id(1)
    qi = qi_tab[p]
    ki = ki_tab[p]
    k_last = (qi * tq + tq - 1) // tk
    _attn_init(ki, m_sc, l_sc, acc_sc)
    crosses_diagonal = ki * tk + tk - 1 > qi * tq

    @pl.when(crosses_diagonal)
    def _():
        rows = qi * tq + lax.broadcasted_iota(jnp.int32, (tq, tk), 0)
        cols = ki * tk + lax.broadcasted_iota(jnp.int32, (tq, tk), 1)
        _attend_heads(q_ref, k_ref, v_ref, m_sc, l_sc, acc_sc, nh, cols <= rows, ck_ref)

    @pl.when(jnp.logical_not(crosses_diagonal))
    def _():
        _attend_heads(q_ref, k_ref, v_ref, m_sc, l_sc, acc_sc, nh, None, ck_ref)

    _attn_finish(ki, k_last, o_ref, l_sc, acc_sc, nh)


def _fox_prompt(q, k, v, logf, tq=512, tk=1024):
    B, S, H, dh = q.shape
    tq, tk = min(tq, S), min(tk, S)
    ck = jnp.swapaxes(jnp.cumsum(logf, axis=1) * LOG2E, 1, 2)[:, :, None, :]
    W = H * dh
    qh = jnp.swapaxes(q * (HEAD_DIM ** -0.5 * LOG2E), 1, 2).astype(_BF16)
    kh = jnp.swapaxes(k, 1, 2).astype(_BF16)
    vh = jnp.swapaxes(v, 1, 2).astype(_BF16)
    qi_tab, ki_tab = _causal_pairs(S, tq, tk)
    grid_spec = pltpu.PrefetchScalarGridSpec(
        num_scalar_prefetch=2,
        grid=(B, int(qi_tab.shape[0])),
        in_specs=[
            pl.BlockSpec((1, H, tq, dh), lambda b, p, qt, kt: (b, 0, qt[p], 0)),
            pl.BlockSpec((1, H, tk, dh), lambda b, p, qt, kt: (b, 0, kt[p], 0)),
            pl.BlockSpec((1, H, tk, dh), lambda b, p, qt, kt: (b, 0, kt[p], 0)),
            pl.BlockSpec((1, H, 1, tk), lambda b, p, qt, kt: (b, 0, 0, kt[p])),
        ],
        out_specs=pl.BlockSpec((1, tq, W), lambda b, p, qt, kt: (b, qt[p], 0)),
        scratch_shapes=[pltpu.VMEM((H, tq, 1), _F32), pltpu.VMEM((H, tq, 1), _F32),
                        pltpu.VMEM((H, tq, dh), _F32)],
    )
    out = pl.pallas_call(
        functools.partial(_fox_attn_kernel, tq=tq, tk=tk, nh=H),
        grid_spec=grid_spec,
        out_shape=jax.ShapeDtypeStruct((B, S, W), _F32),
        compiler_params=pltpu.CompilerParams(
            dimension_semantics=("parallel", "arbitrary"),
            vmem_limit_bytes=V7X_VMEM_LIMIT_BYTES),
        name="fox_prompt_attn",
    )(qi_tab, ki_tab, qh, kh, vh, ck)
    return out


def _sortable_key(x):
    b = pltpu.bitcast(x, jnp.int32)
    return b ^ ((b >> 31) & jnp.int32(0x7FFFFFFF))


def _topk_rank_threshold(count, nrows, topk):
    kf = float(topk)

    def bits_left(carry):
        i, _, c_v = carry
        return jnp.logical_and(i < 32, jnp.max(jnp.where(c_v != kf, 1.0, 0.0)) > 0.0)

    def bit_step(carry):
        i, v, c_v = carry
        cand = v + jnp.left_shift(jnp.int32(1), 31 - i)
        cnt = count(lambda kk, cols: kk >= cand)
        ok = cnt >= kf
        return i + 1, jnp.where(ok, cand, v), jnp.where(ok, cnt, c_v)

    _, v, _ = lax.while_loop(bits_left, bit_step,
                             (jnp.int32(0), jnp.full((nrows, 1), INT_MIN, jnp.int32),
                              jnp.full((nrows, 1), -1.0, _F32)))
    n_gt = count(lambda kk, cols: kk > v)
    n_ge = count(lambda kk, cols: kk >= v)
    need = kf - n_gt
    has_thr = v != jnp.int32(INT_MIN)
    ties = jnp.max(jnp.where(has_thr & (n_ge > kf), 1.0, 0.0)) > 0.0

    def tie_search(_):
        def idx_step(i, m):
            cand = m + jnp.left_shift(jnp.int32(1), 30 - i)
            cnt = count(lambda kk, cols: (kk == v) & (cols < cand))
            return jnp.where(cnt < need, cand, m)
        return lax.fori_loop(0, 31, idx_step, jnp.zeros((nrows, 1), jnp.int32))

    m_last = lax.cond(ties, tie_search, lambda _: jnp.full((nrows, 1), 2 ** 31 - 1, jnp.int32), 0)
    return v, m_last


def _topk_keep(kk, cols, v, m_last):
    return ((kk > v) | ((kk == v) & (cols <= m_last))) & (kk != jnp.int32(INT_MIN))


def _dsa_select_kernel(iq_ref, iw_ref, ikt_ref, mask_ref, key_sc, *, tq, tk, nk, nh, topk):
    qi = pl.program_id(1)
    nkc = (qi * tq + tq - 1) // tk + 1
    rows = qi * tq + lax.broadcasted_iota(jnp.int32, (tq, tk), 0)
    col0 = lax.broadcasted_iota(jnp.int32, (tq, tk), 1)
    iw = iw_ref[0]

    def score_chunk(c, carry):
        kt = ikt_ref[0, c]
        acc = jnp.zeros((tq, tk), _F32)
        for h in range(nh):
            d = jnp.dot(iq_ref[0, h], kt, preferred_element_type=_F32)
            acc = acc + jnp.maximum(d, 0.0) * iw[:, h:h + 1]
        key = _sortable_key(acc)
        key_sc[c] = jnp.where(c * tk + col0 <= rows, key, jnp.int32(INT_MIN))
        return carry

    lax.fori_loop(0, nkc, score_chunk, 0)

    def count(pred):
        def body(c, part):
            hit = jnp.where(pred(key_sc[c], c * tk + col0), 1.0, 0.0)
            for j in range(tk // V7X_LANES):
                part = part + hit[:, j * V7X_LANES:(j + 1) * V7X_LANES]
            return part
        part = lax.fori_loop(0, nkc, body, jnp.zeros((tq, V7X_LANES), _F32))
        return jnp.sum(part, axis=-1, keepdims=True)

    v, m_last = _topk_rank_threshold(count, tq, topk)

    def write_chunk(c, carry):
        sel = _topk_keep(key_sc[c], c * tk + col0, v, m_last)
        mask_ref[0, 0, c] = sel.astype(jnp.int8)
        return carry

    lax.fori_loop(0, nkc, write_chunk, 0)

    def zero_chunk(c, carry):
        mask_ref[0, 0, c] = jnp.zeros((tq, tk), jnp.int8)
        return carry

    lax.fori_loop(nkc, nk, zero_chunk, 0)


def _dsa_select(iq, ik, iw, topk, tq=256, tk=1024):
    B, S, Hi, Di = iq.shape
    tq, tk = min(tq, S), min(tk, S)
    nq, nk = S // tq, S // tk
    iqh = jnp.swapaxes(iq, 1, 2).astype(_BF16)
    ikt = jnp.swapaxes(ik.reshape(B, nk, tk, Di), 2, 3).astype(_BF16)
    iws = (iw * (D_IDX ** -0.5)).astype(_F32)
    return pl.pallas_call(
        functools.partial(_dsa_select_kernel, tq=tq, tk=tk, nk=nk, nh=Hi, topk=topk),
        grid=(B, nq),
        in_specs=[pl.BlockSpec((1, Hi, tq, Di), lambda b, q: (b, 0, q, 0)),
                  pl.BlockSpec((1, tq, Hi), lambda b, q: (b, q, 0)),
                  pl.BlockSpec((1, nk, Di, tk), lambda b, q: (b, 0, 0, 0))],
        out_specs=pl.BlockSpec((1, 1, nk, tq, tk), lambda b, q: (b, q, 0, 0, 0)),
        out_shape=jax.ShapeDtypeStruct((B, nq, nk, tq, tk), jnp.int8),
        scratch_shapes=[pltpu.VMEM((nk, tq, tk), jnp.int32)],
        compiler_params=pltpu.CompilerParams(
            dimension_semantics=("parallel", "arbitrary"),
            vmem_limit_bytes=V7X_VMEM_LIMIT_BYTES),
        name="dsa_topk_select",
    )(iqh, iws, ikt)


def _dsa_attn_kernel(qi_tab, ki_tab, q_ref, k_ref, v_ref, mask_ref, o_ref, m_sc, l_sc, acc_sc,
                     *, tq, tk, nh):
    p = pl.program_id(1)
    qi = qi_tab[p]
    ki = ki_tab[p]
    k_last = (qi * tq + tq - 1) // tk
    _attn_init(ki, m_sc, l_sc, acc_sc)
    keep = mask_ref[0, :, 0].reshape(tq, tk) != 0
    _attend_heads(q_ref, k_ref, v_ref, m_sc, l_sc, acc_sc, nh, keep)
    _attn_finish(ki, k_last, o_ref, l_sc, acc_sc, nh)


def _dsa_prompt(q, k, v, iq, ik, iw, tq=512, tk=1024, tq_sel=128):
    B, S, H, dh = q.shape
    topk = min(TOPK_MAX, S // 4)
    tq, tk, tq_sel = min(tq, S), min(tk, S), min(tq_sel, S)
    mask = _dsa_select(iq, ik, iw, topk, tq=tq_sel, tk=tk)
    r = tq // tq_sel
    W = H * dh
    qh = jnp.swapaxes(q * (HEAD_DIM ** -0.5 * LOG2E), 1, 2).astype(_BF16)
    kh = jnp.swapaxes(k, 1, 2).astype(_BF16)
    vh = jnp.swapaxes(v, 1, 2).astype(_BF16)
    qi_tab, ki_tab = _causal_pairs(S, tq, tk)
    grid_spec = pltpu.PrefetchScalarGridSpec(
        num_scalar_prefetch=2,
        grid=(B, int(qi_tab.shape[0])),
        in_specs=[
            pl.BlockSpec((1, H, tq, dh), lambda b, p, qt, kt: (b, 0, qt[p], 0)),
            pl.BlockSpec((1, H, tk, dh), lambda b, p, qt, kt: (b, 0, kt[p], 0)),
            pl.BlockSpec((1, H, tk, dh), lambda b, p, qt, kt: (b, 0, kt[p], 0)),
            pl.BlockSpec((1, r, 1, tq_sel, tk), lambda b, p, qt, kt: (b, qt[p], kt[p], 0, 0)),
        ],
        out_specs=pl.BlockSpec((1, tq, W), lambda b, p, qt, kt: (b, qt[p], 0)),
        scratch_shapes=[pltpu.VMEM((H, tq, 1), _F32), pltpu.VMEM((H, tq, 1), _F32),
                        pltpu.VMEM((H, tq, dh), _F32)],
    )
    out = pl.pallas_call(
        functools.partial(_dsa_attn_kernel, tq=tq, tk=tk, nh=H),
        grid_spec=grid_spec,
        out_shape=jax.ShapeDtypeStruct((B, S, W), _F32),
        compiler_params=pltpu.CompilerParams(
            dimension_semantics=("parallel", "arbitrary"),
            vmem_limit_bytes=V7X_VMEM_LIMIT_BYTES),
        name="dsa_prompt_attn",
    )(qi_tab, ki_tab, qh, kh, vh, mask)
    return out


def _split2(x):
    hi = x.astype(_BF16)
    lo = (x - hi.astype(_F32)).astype(_BF16)
    return hi, lo


def _dot3(a, b):
    ah, al = a
    bh, bl = b
    return _dot(ah, bh) + _dot(ah, bl) + _dot(al, bh)


def _wkv_chunk_kernel(r_ref, lw_ref, k_ref, v_ref, al_ref, be_ref, s0_ref, y_ref, sT_ref, s_sc,
                      *, L, n, npair, nchunk):
    t_blk = pl.program_id(2)
    L2, n2 = 2 * L, 2 * n

    @pl.when(t_blk == 0)
    def _():
        s_sc[...] = s0_ref[0]

    ri = lax.broadcasted_iota(jnp.int32, (L2, L2), 0)
    ci = lax.broadcasted_iota(jnp.int32, (L2, L2), 1)
    same_head = (ri // L) == (ci // L)
    strict = same_head & (ci < ri)
    incl = same_head & (ci <= ri)
    bs = min(16, L)
    diag_blk = (ri // bs) == (ci // bs)
    eye = (ri == ci).astype(_F32)
    cum_tri = (lax.broadcasted_iota(jnp.int32, (L, L), 0)
               >= lax.broadcasted_iota(jnp.int32, (L, L), 1)).astype(_BF16)
    sr = lax.broadcasted_iota(jnp.int32, (n2, n2), 0)
    sc = lax.broadcasted_iota(jnp.int32, (n2, n2), 1)
    state_blk = (sr // n) == (sc // n)
    lane_a = lax.broadcasted_iota(jnp.int32, (L, n2), 1) < n

    def expand(x):
        return jnp.concatenate([jnp.where(lane_a, x, 0.0), jnp.where(lane_a, 0.0, x)], axis=0)

    def compact(x):
        return x[:L] + x[L:]

    def each(fn, *cols):
        return [fn(*args) for args in zip(*cols)]

    def chunk(c, carry):
        rows = pl.ds(pl.multiple_of(c * L, L), L)
        lane_sl = [slice(p * n2, (p + 1) * n2) for p in range(npair)]
        lw = [lw_ref[0, rows, ls] for ls in lane_sl]
        r = [r_ref[0, rows, ls] for ls in lane_sl]
        k = [k_ref[0, rows, ls] for ls in lane_sl]
        v = [v_ref[0, rows, ls] for ls in lane_sl]
        al = [al_ref[0, rows, ls] for ls in lane_sl]
        be = [be_ref[0, rows, ls] for ls in lane_sl]

        def cumsum(x):
            hi = x.astype(_BF16)
            r1 = x - hi.astype(_F32)
            mid = r1.astype(_BF16)
            lo = (r1 - mid.astype(_F32)).astype(_BF16)
            return _dot(cum_tri, hi) + _dot(cum_tri, mid) + _dot(cum_tri, lo)

        g = each(cumsum, lw)
        g_last = each(lambda x: x[L - 1:L, :], g)
        a_t = each(lambda x, gg, l: x * jnp.exp(gg - l), al, g, lw)
        r_t = each(lambda x, gg: x * jnp.exp(gg), r, g)
        eng = each(lambda gg: jnp.exp(-gg), g)
        pa_pr = each(lambda x, y: jnp.concatenate([expand(x), expand(y)], axis=0).astype(_BF16), a_t, r_t)
        pb = each(lambda x, e: expand(x * e).astype(_BF16), be, eng)
        pk = each(lambda x, e: expand(x * e).astype(_BF16), k, eng)
        pv = each(lambda x: expand(x).astype(_BF16), v)
        xb = each(_dot_nt, pa_pr, pb)
        xk = each(_dot_nt, pa_pr, pk)
        m_ab = each(lambda x: jnp.where(strict, x[:L2], 0.0), xb)
        m_ak = each(lambda x: jnp.where(strict, x[:L2], 0.0).astype(_BF16), xk)
        n_rb = each(lambda x: jnp.where(incl, x[L2:], 0.0).astype(_BF16), xb)
        n_rk = each(lambda x: jnp.where(incl, x[L2:], 0.0).astype(_BF16), xk)
        d1 = each(lambda x: jnp.where(diag_blk, x, 0.0), m_ab)
        e_b = each(lambda x, d: (x - d).astype(_BF16), m_ab, d1)
        d1s = each(_split2, d1)
        d2 = each(_dot3, d1s, d1s)
        d2s = each(_split2, d2)
        d4 = each(_dot3, d2s, d2s)
        d4s = each(_split2, d4)
        d8 = each(_dot3, d4s, d4s)
        p12 = each(lambda x, y: _dot3(_split2(eye + x), _split2(eye + y)), d1, d2)
        p48 = each(lambda x, y: _dot3(_split2(eye + x), _split2(eye + y)), d4, d8)
        x_b = each(lambda x, y: _dot3(_split2(x), _split2(y)).astype(_BF16), p12, p48)
        nn = each(_dot, x_b, e_b)
        nn_b = each(lambda x: x.astype(_BF16), nn)
        nn2 = each(_dot, nn_b, nn_b)
        q12 = each(lambda x, y: _dot((eye + x).astype(_BF16), (eye + y).astype(_BF16)), nn, nn2)
        tinv = each(lambda x, y: _dot(x.astype(_BF16), y).astype(_BF16), q12, x_b)
        s = [s_sc[p] for p in range(npair)]
        ars = each(lambda x, y, st: _dot_nt(jnp.concatenate([x, y], axis=0).astype(_BF16), st.astype(_BF16)),
                   a_t, r_t, s)
        rhs = each(lambda x, m, vv: (expand(x[:L]) + _dot(m, vv)).astype(_BF16), ars, m_ak, pv)
        u_e = each(_dot, tinv, rhs)
        y_e = each(lambda x, nb, u, nk, vv: expand(x[L:]) + _dot(nb, u.astype(_BF16)) + _dot(nk, vv),
                   ars, n_rb, u_e, n_rk, pv)
        for ls, ye in zip(lane_sl, y_e):
            y_ref[0, rows, ls] = compact(ye)
        dec = each(lambda gl, gg: jnp.exp(gl - gg), g_last, g)
        uv = each(lambda u, vv: jnp.concatenate([compact(u), vv], axis=0).astype(_BF16), u_e, v)
        bk = each(lambda x, y, d: jnp.concatenate([x * d, y * d], axis=0).astype(_BF16), be, k, dec)
        upd = each(_dot_tn, uv, bk)
        for p in range(npair):
            s_sc[p] = s[p] * jnp.exp(g_last[p]) + jnp.where(state_blk, upd[p], 0.0)
        return carry

    lax.fori_loop(0, nchunk, chunk, 0)

    @pl.when(t_blk == pl.num_programs(2) - 1)
    def _():
        sT_ref[0] = s_sc[...]


def _wkv7(s0, r, lw, k, v, kk, a, L=64, tc=256, npair=8):
    B, T, H, N = r.shape
    L, tc = min(L, T), min(tc, T)
    npair = min(npair, H // 2)
    D = H * N
    flat = lambda x: x.reshape(B, T, D)
    s0p = s0.reshape(B, H // 2, 2, N, N)
    z = jnp.zeros_like(s0p[:, :, 0])
    s0bd = jnp.concatenate([jnp.concatenate([s0p[:, :, 0], z], axis=-1),
                            jnp.concatenate([z, s0p[:, :, 1]], axis=-1)], axis=-2)
    seq_spec = pl.BlockSpec((1, tc, npair * 2 * N), lambda b, g, t: (b, t, g))
    st_spec = pl.BlockSpec((1, npair, 2 * N, 2 * N), lambda b, g, t: (b, g, 0, 0))
    y, sT = pl.pallas_call(
        functools.partial(_wkv_chunk_kernel, L=L, n=N, npair=npair, nchunk=tc // L),
        grid=(B, H // (2 * npair), T // tc),
        in_specs=[seq_spec] * 6 + [st_spec],
        out_specs=[seq_spec, st_spec],
        out_shape=[jax.ShapeDtypeStruct((B, T, D), _F32),
                   jax.ShapeDtypeStruct((B, H // 2, 2 * N, 2 * N), _F32)],
        scratch_shapes=[pltpu.VMEM((npair, 2 * N, 2 * N), _F32)],
        compiler_params=pltpu.CompilerParams(
            dimension_semantics=("parallel", "parallel", "arbitrary"),
            vmem_limit_bytes=V7X_VMEM_LIMIT_BYTES),
        name="wkv7_chunked",
    )(flat(r), flat(lw), flat(k), flat(v), flat(-kk), flat(kk * a), s0bd)
    sT = jnp.stack([sT[:, :, :N, :N], sT[:, :, N:, N:]], axis=2).reshape(B, H, N, N)
    return sT, y.reshape(B, T, H, N)


def _rmsnorm(x, g):
    y = x * lax.rsqrt(jnp.mean(x * x, axis=-1, keepdims=True) + RMS_EPS)
    return y * g


def _partial_rope(x, pos):
    half = ROPE_DIM // 2
    inv = ROPE_THETA ** (-2.0 * jnp.arange(half, dtype=_F32) / ROPE_DIM)
    ang = pos.astype(_F32)[:, None] * inv
    cos, sin = jnp.cos(ang)[:, None, :], jnp.sin(ang)[:, None, :]
    x1, x2 = x[..., :half], x[..., half:ROPE_DIM]
    return jnp.concatenate([x1 * cos - x2 * sin, x2 * cos + x1 * sin, x[..., ROPE_DIM:]], axis=-1)


def _even_project(hn, w_in, b_forget, pos):
    B, T, _ = hn.shape
    fq, fk, fv, ff, fg, dq, dk, dv, dg, iq, ik, iw = _even_proj(hn, w_in)

    def heads(t, h):
        return t.reshape(B, T, h, -1)

    logf = jax.nn.log_sigmoid(ff + b_forget)
    dq = _partial_rope(heads(dq, H_DSA), pos)
    dk = _partial_rope(heads(dk, H_DSA), pos)
    iq = _partial_rope(heads(iq, H_IDX), pos)
    ik = _partial_rope(ik[:, :, None, :], pos)[:, :, 0]
    iw = iw * (H_IDX ** -0.5)
    return (heads(fq, H_FOX), heads(fk, H_FOX), heads(fv, H_FOX), logf, jax.nn.silu(fg),
            dq, dk, heads(dv, H_DSA), jax.nn.silu(dg), iq, ik, iw)


PAGES_PER_STEP = 16
NEW_ROWS_PAD = 16


def _paged_attn_kernel(pt_ref, q_ref, roff_ref, bias_ref, *refs, pg, page):
    k_refs, v_refs = refs[:pg], refs[pg:2 * pg]
    kn_ref, vn_ref, bn_ref, o_ref, m_sc, l_sc, acc_sc = refs[2 * pg:]
    j = pl.program_id(1)

    @pl.when(j == 0)
    def _():
        m_sc[...] = jnp.full(m_sc.shape, NEG_BIG, _F32)
        l_sc[...] = jnp.zeros(l_sc.shape, _F32)
        acc_sc[...] = jnp.zeros(acc_sc.shape, _F32)

    q = q_ref[0]
    roff = roff_ref[0]
    nrep = q.shape[0] // bias_ref.shape[1]

    def update(scores, values, weigh):
        m_old = m_sc[...]
        m_new = m_old
        for s in scores:
            m_new = jnp.maximum(m_new, jnp.max(s, axis=-1, keepdims=True))
        alpha = jnp.exp(m_old - m_new)
        l_new = alpha * l_sc[...]
        acc = alpha * acc_sc[...]
        for s, v in zip(scores, values):
            p = jnp.exp(s - m_new)
            l_new = l_new + jnp.sum(p, axis=-1, keepdims=True)
            acc = acc + weigh(p.astype(_BF16), v)
        l_sc[...] = l_new
        acc_sc[...] = acc
        m_sc[...] = m_new

    def page_t(ref):
        return ref[0, 0].reshape(q.shape[1], page).astype(_BF16)

    scores = []
    for i in range(pg):
        b = bias_ref[0, :, i * page:(i + 1) * page]
        scores.append(_dot(q, page_t(k_refs[i])) + roff + jnp.concatenate([b] * nrep, axis=0))
    update(scores, [page_t(v_refs[i]) for i in range(pg)], _dot_nt)

    @pl.when(j == pl.num_programs(1) - 1)
    def _():
        update([_dot_nt(q, kn_ref[0].astype(_BF16)) + bn_ref[0]], [vn_ref[0].astype(_BF16)], _dot)
        o_ref[0] = acc_sc[...] / l_sc[...]


def _paged_attention(qbd, roff, bias_t, cache_k, cache_v, page_table, layer, k_new, v_new, bias_new):
    B, R, W = qbd.shape
    nh, dh = cache_k.shape[3:]
    cache_k = jnp.transpose(cache_k, (0, 1, 3, 4, 2))
    cache_v = jnp.transpose(cache_v, (0, 1, 3, 4, 2))
    npages = page_table.shape[1]
    pg = min(PAGES_PER_STEP, npages)
    assert npages % pg == 0
    tn = k_new.shape[1]
    tile_rows = bias_t.shape[1]

    def page_spec(i):
        return pl.BlockSpec((1, 1, nh, dh, PAGE_SIZE),
                            lambda b, j, pt, i=i: (pt[b * npages + j * pg + i], layer, 0, 0, 0))

    grid_spec = pltpu.PrefetchScalarGridSpec(
        num_scalar_prefetch=1,
        grid=(B, npages // pg),
        in_specs=[pl.BlockSpec((1, R, W), lambda b, j, pt: (b, 0, 0)),
                  pl.BlockSpec((1, R, 1), lambda b, j, pt: (b, 0, 0)),
                  pl.BlockSpec((1, tile_rows, pg * PAGE_SIZE), lambda b, j, pt: (b, 0, j))]
        + [page_spec(i) for i in range(pg)] + [page_spec(i) for i in range(pg)]
        + [pl.BlockSpec((1, tn, W), lambda b, j, pt: (b, 0, 0)),
           pl.BlockSpec((1, tn, W), lambda b, j, pt: (b, 0, 0)),
           pl.BlockSpec((1, R, tn), lambda b, j, pt: (b, 0, 0))],
        out_specs=pl.BlockSpec((1, R, W), lambda b, j, pt: (b, 0, 0)),
        scratch_shapes=[pltpu.VMEM((R, 1), _F32), pltpu.VMEM((R, 1), _F32), pltpu.VMEM((R, W), _F32)],
    )
    return pl.pallas_call(
        functools.partial(_paged_attn_kernel, pg=pg, page=PAGE_SIZE),
        grid_spec=grid_spec,
        out_shape=jax.ShapeDtypeStruct((B, R, W), _F32),
        compiler_params=pltpu.CompilerParams(
            dimension_semantics=("parallel", "arbitrary"),
            vmem_limit_bytes=V7X_VMEM_LIMIT_BYTES),
        name="paged_decode_attn",
    )(page_table.reshape(-1), qbd, roff, bias_t, *([cache_k] * pg), *([cache_v] * pg), k_new, v_new, bias_new)


def _pad_rows(x, rows, value=0.0):
    return jnp.pad(x, ((0, 0), (0, rows - x.shape[1]), (0, 0)), constant_values=value)


def _fox_sample(q, k, v, logf, cache_k, cache_v, cache_logf, page_table, layer):
    B, T, H, dh = q.shape
    W = H * dh
    logf_past = cache_logf[page_table, layer].reshape(B, -1, H)
    c_past = jnp.cumsum(logf_past, axis=1)
    c_new = c_past[:, -1:] + jnp.cumsum(logf, axis=1)
    eye = jnp.eye(H, dtype=_F32)
    qs = q * (HEAD_DIM ** -0.5)
    qbd = (qs[:, :, :, None, :] * eye[None, None, :, :, None]).reshape(B, T * H, W).astype(_BF16)
    roff = c_new.reshape(B, T * H, 1)
    bias_t = -jnp.swapaxes(c_past, 1, 2)
    tri = jnp.arange(T)[None, :] <= jnp.arange(T)[:, None]
    bn = c_new[:, :, None, :] - c_new[:, None, :, :]
    bn = jnp.where(tri[None, :, :, None], bn, NEG_BIG)
    bn = jnp.swapaxes(bn, 2, 3).reshape(B, T * H, T)
    bn = jnp.pad(bn, ((0, 0), (0, 0), (0, NEW_ROWS_PAD - T)), constant_values=NEG_BIG)
    out = _paged_attention(qbd, roff, bias_t, cache_k, cache_v, page_table, layer,
                           _pad_rows(k.reshape(B, T, W), NEW_ROWS_PAD),
                           _pad_rows(v.reshape(B, T, W), NEW_ROWS_PAD), bn)
    out = out.reshape(B, T, H, H, dh)
    return jnp.stack([out[:, :, h, h] for h in range(H)], axis=2).reshape(B, T, W)


def _dsa_sample_select_kernel(pt_ref, x_ref, w_ref, kn_ref, *refs, npages, page, nq, nh, topk):
    kp_refs = refs[:npages]
    bias_ref, key_sc = refs[npages:]
    i = pl.program_id(1)
    x = x_ref[0]
    w = w_ref[0]
    r0 = pl.multiple_of(i * nq, nq)

    def fold(d):
        t = jnp.maximum(d, 0.0) * w
        acc = t[0:nq]
        for h in range(1, nh):
            acc = acc + t[h * nq:(h + 1) * nq]
        return acc

    for p in range(npages):
        d = _dot(x, kp_refs[p][0, 0].astype(_BF16))
        key_sc[pl.ds(r0, nq), p * page:(p + 1) * page] = _sortable_key(fold(d))
    dn = _dot_nt(x, kn_ref[0].astype(_BF16))
    qrow = lax.broadcasted_iota(jnp.int32, (nq, page), 0)
    col = lax.broadcasted_iota(jnp.int32, (nq, page), 1)
    key_sc[pl.ds(r0, nq), npages * page:(npages + 1) * page] = jnp.where(
        col <= qrow, _sortable_key(fold(dn)), jnp.int32(INT_MIN))

    @pl.when(i == pl.num_programs(1) - 1)
    def _():
        nrows, ncols = key_sc.shape
        cols = lax.broadcasted_iota(jnp.int32, (nrows, ncols), 1)

        def count(pred):
            return jnp.sum(jnp.where(pred(key_sc[...], cols), 1.0, 0.0), axis=-1, keepdims=True)

        v, m_last = _topk_rank_threshold(count, nrows, topk)
        bias_ref[0] = jnp.where(_topk_keep(key_sc[...], cols, v, m_last), 0.0, NEG_BIG)


def _dsa_sample_select(iq, ik, iw, cache_kidx, page_table, layer, topk, gb=8):
    B, T, Hi, Di = iq.shape
    npages = page_table.shape[1]
    gb = min(gb, B)
    assert B % gb == 0
    ncols = (npages + 1) * PAGE_SIZE
    x = jnp.swapaxes(iq, 1, 2).reshape(B, Hi * T, Di).astype(_BF16)
    w = jnp.swapaxes(iw * (D_IDX ** -0.5), 1, 2).reshape(B, Hi * T, 1)
    kn = _pad_rows(ik, PAGE_SIZE)

    def page_spec(p):
        return pl.BlockSpec((1, 1, Di, PAGE_SIZE),
                            lambda g, i, pt, p=p: (pt[(g * gb + i) * npages + p], layer, 0, 0))

    grid_spec = pltpu.PrefetchScalarGridSpec(
        num_scalar_prefetch=1,
        grid=(B // gb, gb),
        in_specs=[pl.BlockSpec((1, Hi * T, Di), lambda g, i, pt: (g * gb + i, 0, 0)),
                  pl.BlockSpec((1, Hi * T, 1), lambda g, i, pt: (g * gb + i, 0, 0)),
                  pl.BlockSpec((1, PAGE_SIZE, Di), lambda g, i, pt: (g * gb + i, 0, 0))]
        + [page_spec(p) for p in range(npages)],
        out_specs=pl.BlockSpec((1, gb * T, ncols), lambda g, i, pt: (g, 0, 0)),
        scratch_shapes=[pltpu.VMEM((gb * T, ncols), jnp.int32)],
    )
    bias = pl.pallas_call(
        functools.partial(_dsa_sample_select_kernel, npages=npages, page=PAGE_SIZE, nq=T, nh=Hi, topk=topk),
        grid_spec=grid_spec,
        out_shape=jax.ShapeDtypeStruct((B // gb, gb * T, ncols), _F32),
        compiler_params=pltpu.CompilerParams(
            dimension_semantics=("parallel", "arbitrary"),
            vmem_limit_bytes=V7X_VMEM_LIMIT_BYTES),
        name="dsa_sample_select",
    )(page_table.reshape(-1), x, w, kn, *([jnp.swapaxes(cache_kidx, 2, 3)] * npages))
    return bias.reshape(B, T, ncols)


def _dsa_sample(q, k, v, iq, ik, iw, cache_k, cache_v, cache_kidx, page_table, layer):
    B, T, H, dh = q.shape
    W = H * dh
    P = page_table.shape[1] * PAGE_SIZE
    topk = min(TOPK_MAX, (P + T) // 4)
    sel = _dsa_sample_select(iq, ik, iw, cache_kidx, page_table, layer, topk)
    eye = jnp.eye(H, dtype=_F32)
    qh = jnp.swapaxes(q * (HEAD_DIM ** -0.5), 1, 2)
    qbd = (qh[:, :, :, None, :] * eye[None, :, None, :, None]).reshape(B, H * T, W).astype(_BF16)
    roff = jnp.zeros((B, H * T, 1), _F32)
    bn = jnp.tile(sel[:, :, P:P + NEW_ROWS_PAD], (1, H, 1))
    out = _paged_attention(qbd, roff, sel[:, :, :P], cache_k, cache_v, page_table, layer,
                           _pad_rows(k.reshape(B, T, W), NEW_ROWS_PAD),
                           _pad_rows(v.reshape(B, T, W), NEW_ROWS_PAD), bn)
    out = out.reshape(B, H, T, H, dh)
    return jnp.stack([out[:, h, :, h] for h in range(H)], axis=2).reshape(B, T, W)


def _rwkv_mix(xn, x_prev, s0, v_first, vres, mu, w_rkvg, w_o, w_d0, w_d1, w_d2, w_a0, w_a1, w_a2,
              k_k, k_a, r_k, ln_w, ln_b):
    B, T, D = xn.shape
    H = D // HEAD_DIM
    dx = jnp.concatenate([x_prev[:, None, :], xn[:, :-1]], axis=1) - xn
    xr, xw, xk, xv, xa, xg = [xn + dx * mu[j] for j in range(6)]
    r = _mm3(xr, w_rkvg[0])
    k = _mm3(xk, w_rkvg[1])
    v = _mm3(xv, w_rkvg[2])
    gate = jax.nn.silu(_mm3(xg, w_rkvg[3]))
    w_log = -jax.nn.softplus(-(w_d0 + _mm3(jnp.tanh(_mm3(xw, w_d1)), w_d2))) - 0.5
    log_decay = -jnp.exp(w_log)
    a = jax.nn.sigmoid(w_a0 + _mm3(_mm3(xa, w_a1), w_a2))
    if vres is None:
        v_first = v
    else:
        v0, v1, v2 = vres
        v = v + (v_first - v) * jax.nn.sigmoid(v0 + _mm3(_mm3(xv, v1), v2))

    def heads(t):
        return t.reshape(B, T, H, HEAD_DIM)

    r_h, k_h, v_h, a_h, w_h = heads(r), heads(k), heads(v), heads(a), heads(log_decay)
    kk = heads(k * k_k)
    kk = kk * lax.rsqrt(jnp.maximum(jnp.sum(kk * kk, axis=-1, keepdims=True), 1e-24))
    k_h = k_h * (1.0 + (a_h - 1.0) * k_a.reshape(H, HEAD_DIM))
    s_new, y = _wkv7(s0, r_h, w_h, k_h, v_h, kk, a_h)
    mean = jnp.mean(y, axis=-1, keepdims=True)
    var = jnp.mean(jnp.square(y - mean), axis=-1, keepdims=True)
    y = ((y - mean) * lax.rsqrt(var + GN_EPS) * ln_w.reshape(H, HEAD_DIM) + ln_b.reshape(H, HEAD_DIM))
    y = y + jnp.sum(r_h * k_h * r_k, axis=-1, keepdims=True) * v_h
    out = _mm3(y.reshape(B, T, D) * gate, w_o)
    return out, s_new, xn[:, -1], v_first


def _run_group(x, ple, ctx, prm):
    B, T, D = x.shape
    depth = prm['g_mix'].shape[0]
    past = 0 if ctx is None else ctx['page_table'].shape[1] * PAGE_SIZE
    pos = past + jnp.arange(T, dtype=jnp.int32)
    h = x
    v_first = None
    even_rows, odd_rows = [], []
    for i in range(depth):
        hn = _rmsnorm(h, prm['g_mix'][i])
        if i % 2 == 0:
            e = i // 2
            fq, fk, fv, logf, fg, dq, dk, dv, dg, iq, ik, iw = _even_project(
                hn, prm['w_in'][e], prm['b_forget'][e], pos)
            if ctx is None:
                o_f = _fox_prompt(fq, fk, fv, logf)
                o_d = _dsa_prompt(dq, dk, dv, iq, ik, iw)
            else:
                pt = ctx['page_table']
                o_f = _fox_sample(fq, fk, fv, logf, ctx['cache_k_fox'], ctx['cache_v_fox'],
                                  ctx['cache_logf_fox'], pt, e)
                o_d = _dsa_sample(dq, dk, dv, iq, ik, iw, ctx['cache_k_dsa'], ctx['cache_v_dsa'],
                                  ctx['cache_kidx_dsa'], pt, e)
            mixed = jnp.concatenate([o_f * fg, o_d * dg], axis=-1)
            h = h + _mm3(mixed, prm['w_out'][e])
            even_rows.append((fk, fv, logf, dk, dv, ik))
        else:
            o = i // 2
            if ctx is None:
                s0 = jnp.zeros((B, D // HEAD_DIM, HEAD_DIM, HEAD_DIM), x.dtype)
                x_prev = jnp.zeros((B, D), x.dtype)
            else:
                s0, x_prev = ctx['state_wkv'][o], ctx['state_shift'][o]
            vres = None if o == 0 else (prm['w_v0'][o - 1], prm['w_v1'][o - 1], prm['w_v2'][o - 1])
            out, s_new, shift_new, v_first = _rwkv_mix(
                hn, x_prev, s0, v_first, vres, prm['mu_rwkv'][o], prm['w_rkvg'][o], prm['w_o_rwkv'][o],
                prm['w_decay0'][o], prm['w_decay1'][o], prm['w_decay2'][o], prm['w_a0'][o], prm['w_a1'][o],
                prm['w_a2'][o], prm['k_k'][o], prm['k_a'][o], prm['r_k'][o], prm['ln_x_w'][o],
                prm['ln_x_b'][o])
            h = h + out
            odd_rows.append((s_new, shift_new))
        gate = jax.nn.sigmoid(_mm3(_rmsnorm(h, prm['g_ple'][i]), prm['w_ple_gate'][i]))
        h = h + gate * _mm3(ple[i], prm['w_ple_proj'][i])
    y = _rmsnorm(h, prm['g_final'])
    new = [jnp.stack([r[j] for r in even_rows], axis=1) for j in range(6)]
    new += [jnp.stack([r[j] for r in odd_rows], axis=0) for j in range(2)]
    return y, new


def kernel(x_prompt, x_sample, cache_k_fox, cache_v_fox, cache_logf_fox, cache_k_dsa, cache_v_dsa, cache_kidx_dsa, state_wkv, state_shift, page_table, p_prompt, p_sample, g_mix, w_in, b_forget, w_out, mu_rwkv, w_rkvg, w_o_rwkv, w_decay0, w_decay1, w_decay2, w_a0, w_a1, w_a2, w_v0, w_v1, w_v2, k_k, k_a, r_k, ln_x_w, ln_x_b, g_ple, w_ple_gate, w_ple_proj, g_final):
    prm = {'g_mix': g_mix, 'w_in': w_in, 'b_forget': b_forget, 'w_out': w_out, 'mu_rwkv': mu_rwkv,
           'w_rkvg': w_rkvg, 'w_o_rwkv': w_o_rwkv, 'w_decay0': w_decay0, 'w_decay1': w_decay1,
           'w_decay2': w_decay2, 'w_a0': w_a0, 'w_a1': w_a1, 'w_a2': w_a2, 'w_v0': w_v0, 'w_v1': w_v1,
           'w_v2': w_v2, 'k_k': k_k, 'k_a': k_a, 'r_k': r_k, 'ln_x_w': ln_x_w, 'ln_x_b': ln_x_b,
           'g_ple': g_ple, 'w_ple_gate': w_ple_gate, 'w_ple_proj': w_ple_proj, 'g_final': g_final}
    y_prompt, new_p = _run_group(x_prompt, p_prompt, None, prm)
    ctx = {'page_table': page_table, 'cache_k_fox': cache_k_fox, 'cache_v_fox': cache_v_fox,
           'cache_logf_fox': cache_logf_fox, 'cache_k_dsa': cache_k_dsa, 'cache_v_dsa': cache_v_dsa,
           'cache_kidx_dsa': cache_kidx_dsa, 'state_wkv': state_wkv, 'state_shift': state_shift}
    y_sample, new_s = _run_group(x_sample, p_sample, ctx, prm)
    return (y_prompt, y_sample) + tuple(new_p) + tuple(new_s)
```
